```python
import math, functools
import jax, jax.numpy as jnp
from jax import lax
import numpy as np

D_MODEL = 4096
BATCH = 2
SEQ = 4096
DEPTH = 1
DEC_BATCH = 128
DEC_SEQ = 1
PAST_LEN = 2048
PAGE_SIZE = 128

D_SSM = D_MODEL // 2
SSM_GROUP = 16
N_GROUPS = D_SSM // SSM_GROUP
SSM_STATE = 64
SB_HEAD_DIM = 128
D_SB = D_MODEL // 2
SB_HEADS = D_SB // SB_HEAD_DIM
SB_BIAS_INIT = -6.5
N_MEM = 256
MEM_HEADS = 4
MEM_HEAD_DIM = D_MODEL // MEM_HEADS
D_FF = 4 * D_MODEL
Q_BLOCK = 128
LN_EPS = 1e-5
DN_ALPHA = (2 * DEPTH) ** 0.25
DN_BETA = (8 * DEPTH) ** -0.25
D_IN = D_SSM + 3 * D_SB + 2 * D_MODEL
SPLITS = [D_SSM, D_SSM + D_SB, D_SSM + 2 * D_SB, D_SSM + 3 * D_SB, D_SSM + 3 * D_SB + D_MODEL]
F32 = jnp.float32

kernel_name = 'hybrid_s5_stickbreaking_decoder_step'


def layer_norm(x, g, b):
    xf = x.astype(F32)
    mu = jnp.mean(xf, -1, keepdims=True)
    var = jnp.mean(jnp.square(xf - mu), -1, keepdims=True)
    return ((xf - mu) * lax.rsqrt(var + LN_EPS) * g.astype(F32) + b.astype(F32)).astype(x.dtype)


def post_norm(x, f, g, b):
    return layer_norm(DN_ALPHA * x + f, g, b)


def s5_discretise(a_re, a_im, log_dt, b_re, b_im):
    a_re, a_im = a_re.astype(F32), a_im.astype(F32)
    b_re, b_im = b_re.astype(F32), b_im.astype(F32)
    dt = jnp.exp(log_dt.astype(F32))[:, None]
    ld_re, ld_im = a_re * dt, a_im * dt
    mag = jnp.exp(ld_re)
    lb_re, lb_im = mag * jnp.cos(ld_im), mag * jnp.sin(ld_im)
    den = a_re * a_re + a_im * a_im
    f_re = ((lb_re - 1.0) * a_re + lb_im * a_im) / den
    f_im = (lb_im * a_re - (lb_re - 1.0) * a_im) / den
    bb_re = f_re[..., None] * b_re - f_im[..., None] * b_im
    bb_im = f_re[..., None] * b_im + f_im[..., None] * b_re
    return ld_re, ld_im, lb_re, lb_im, bb_re, bb_im


def _ssm_combine(e1, e2):
    a1r, a1i, b1r, b1i = e1
    a2r, a2i, b2r, b2i = e2
    return (a1r * a2r - a1i * a2i, a1r * a2i + a1i * a2r,
            a2r * b1r - a2i * b1i + b2r, a2r * b1i + a2i * b1r + b2i)


def s5_scan(u, disc, c_re, c_im, d_skip, h0_re, h0_im):
    ld_re, ld_im, lb_re, lb_im, bb_re, bb_im = disc
    t = u.shape[1]
    bu_re = jnp.einsum('ntgc,gpc->ntgp', u, bb_re)
    bu_im = jnp.einsum('ntgc,gpc->ntgp', u, bb_im)
    a_re = jnp.broadcast_to(lb_re, (1, t) + lb_re.shape)
    a_im = jnp.broadcast_to(lb_im, (1, t) + lb_im.shape)
    _, _, h_re, h_im = lax.associative_scan(_ssm_combine, (a_re, a_im, bu_re, bu_im), axis=1)
    if h0_re is not None:
        steps = jnp.arange(1, t + 1, dtype=F32)[:, None, None]
        mag = jnp.exp(steps * ld_re)
        p_re, p_im = mag * jnp.cos(steps * ld_im), mag * jnp.sin(steps * ld_im)
        s_re, s_im = h0_re.astype(F32)[:, None], h0_im.astype(F32)[:, None]
        h_re = h_re + p_re * s_re - p_im * s_im
        h_im = h_im + p_re * s_im + p_im * s_re
    y = (jnp.einsum('ntgp,gcp->ntgc', h_re, c_re.astype(F32))
         - jnp.einsum('ntgp,gcp->ntgc', h_im, c_im.astype(F32))
         + d_skip.astype(F32) * u)
    return y, h_re[:, -1], h_im[:, -1]


def stick_breaking_weights(z, bias, q_pos, k_pos):
    z = z * (SB_HEAD_DIM ** -0.5) + bias.astype(F32)[:, None, None]
    mask = k_pos[None, :] < q_pos[:, None]
    log_keep = jnp.where(mask, jax.nn.log_sigmoid(-z), 0.0)
    suffix = lax.cumsum(log_keep, axis=z.ndim - 1, reverse=True)
    after = jnp.concatenate([suffix[..., 1:], jnp.zeros_like(suffix[..., :1])], axis=-1)
    return jnp.where(mask, jnp.exp(jax.nn.log_sigmoid(z) + after), 0.0)


def sb_prompt(q, k, v, bias):
    n, t, h, d = q.shape
    nb = t // Q_BLOCK
    q_blocks = jnp.moveaxis(q.reshape(n, nb, Q_BLOCK, h, d), 1, 0)
    k_pos = jnp.arange(t, dtype=jnp.int32)

    def one_block(args):
        qb, bi = args
        q_pos = bi * Q_BLOCK + jnp.arange(Q_BLOCK, dtype=jnp.int32)
        z = jnp.einsum('nqhd,nkhd->nhqk', qb, k, preferred_element_type=F32)
        w = stick_breaking_weights(z, bias, q_pos, k_pos)
        return jnp.einsum('nhqk,nkhd->nqhd', w.astype(v.dtype), v)

    out = lax.map(one_block, (q_blocks, jnp.arange(nb, dtype=jnp.int32)))
    return jnp.moveaxis(out, 0, 1).reshape(n, t, h * d)


def sb_sample(q, k, v, bias, pool_k, pool_v, layer, page_table):
    n, t, h, d = q.shape
    past = page_table.shape[1] * pool_k.shape[2]
    k_past = pool_k[layer][page_table].reshape(n, past, h, d)
    v_past = pool_v[layer][page_table].reshape(n, past, h, d)
    z = jnp.concatenate([jnp.einsum('nqhd,nkhd->nhqk', q, k_past, preferred_element_type=F32),
                         jnp.einsum('nqhd,nkhd->nhqk', q, k, preferred_element_type=F32)], axis=-1)
    q_pos = past + jnp.arange(t, dtype=jnp.int32)
    k_pos = jnp.arange(past + t, dtype=jnp.int32)
    w = stick_breaking_weights(z, bias, q_pos, k_pos).astype(v.dtype)
    out = (jnp.einsum('nhqk,nkhd->nqhd', w[..., :past], v_past)
           + jnp.einsum('nhqk,nkhd->nqhd', w[..., past:], v))
    return out.reshape(n, t, h * d)


def mixing_sublayer(x, w_in, disc, c_re, c_im, d_skip, w_glu_val, w_glu_gate,
                    w_sb_branch, w_mix_out, ln_g, ln_b, sb_fn, h0_re, h0_im):
    n, t, _ = x.shape
    u, q, k, v, g_ssm, g_sb = jnp.split(x @ w_in, SPLITS, axis=-1)
    q = q.reshape(n, t, SB_HEADS, SB_HEAD_DIM)
    k = k.reshape(n, t, SB_HEADS, SB_HEAD_DIM)
    v = v.reshape(n, t, SB_HEADS, SB_HEAD_DIM)
    y, h_re, h_im = s5_scan(u.reshape(n, t, N_GROUPS, SSM_GROUP).astype(F32), disc,
                            c_re, c_im, d_skip, h0_re, h0_im)
    y = jax.nn.gelu(y).astype(x.dtype).reshape(n, t, D_SSM)
    ssm_out = (y @ w_glu_val) * jax.nn.sigmoid(y @ w_glu_gate)
    sb_out = sb_fn(q, k, v) @ w_sb_branch
    merged = jax.nn.sigmoid(g_ssm) * ssm_out + jax.nn.sigmoid(g_sb) * sb_out
    x_new = post_norm(x, merged @ w_mix_out, ln_g, ln_b)
    return x_new, k, v, h_re, h_im


def memory_kv(mem, w_k, w_v):
    n, m, _ = mem.shape
    return ((mem @ w_k).reshape(n, m, MEM_HEADS, MEM_HEAD_DIM),
            (mem @ w_v).reshape(n, m, MEM_HEADS, MEM_HEAD_DIM))


def cross_attention(x, mem_k, mem_v, w_q, w_o):
    n, t, _ = x.shape
    q = (x @ w_q).reshape(n, t, MEM_HEADS, MEM_HEAD_DIM)
    s = jnp.einsum('nthd,nmhd->nhtm', q, mem_k, preferred_element_type=F32) * (MEM_HEAD_DIM ** -0.5)
    p = jax.nn.softmax(s, axis=-1).astype(mem_v.dtype)
    o = jnp.einsum('nhtm,nmhd->nthd', p, mem_v).reshape(n, t, D_MODEL)
    return o @ w_o


def sq_relu_mlp(x, w1, w2):
    return jnp.square(jax.nn.relu(x @ w1)) @ w2


def setup_inputs(seed: int = 0) -> dict:
    key = jax.random.key(seed)
    ks = iter(jax.random.split(key, 64))

    def nrm(shape, scale):
        return jax.random.normal(next(ks), shape, F32) * scale

    n_pages = PAST_LEN // PAGE_SIZE
    n_used = DEC_BATCH * n_pages
    n_pool = n_used + n_used // 4
    page_table = jax.random.permutation(next(ks), n_pool)[:n_used].reshape(DEC_BATCH, n_pages).astype(jnp.int32)
    sd = D_MODEL ** -0.5
    w_in = jnp.concatenate([
        nrm((DEPTH, D_MODEL, D_SSM), sd),
        nrm((DEPTH, D_MODEL, 2 * D_SB), sd),
        nrm((DEPTH, D_MODEL, D_SB), sd * DN_BETA),
        nrm((DEPTH, D_MODEL, 2 * D_MODEL), sd),
    ], axis=-1)
    a_re = -0.5 + nrm((DEPTH, N_GROUPS, SSM_STATE), 0.01)
    a_im = math.pi * jnp.arange(SSM_STATE, dtype=F32) + nrm((DEPTH, N_GROUPS, SSM_STATE), 0.01)
    log_dt = math.log(1e-3) + jax.random.uniform(next(ks), (DEPTH, N_GROUPS), F32) * (math.log(1e-1) - math.log(1e-3))

    def ln_pair():
        return 1.0 + nrm((DEPTH, D_MODEL), 0.02), nrm((DEPTH, D_MODEL), 0.02)

    ln1_g, ln1_b = ln_pair()
    ln2_g, ln2_b = ln_pair()
    ln3_g, ln3_b = ln_pair()
    return {
        'x_prompt': nrm((BATCH, SEQ, D_MODEL), 1.0),
        'x_sample': nrm((DEC_BATCH, DEC_SEQ, D_MODEL), 1.0),
        'cache_k': nrm((DEPTH, n_pool, PAGE_SIZE, SB_HEADS, SB_HEAD_DIM), 1.0),
        'cache_v': nrm((DEPTH, n_pool, PAGE_SIZE, SB_HEADS, SB_HEAD_DIM), 1.0),
        'cache_mem_k': nrm((DEPTH, DEC_BATCH, N_MEM, MEM_HEADS, MEM_HEAD_DIM), 1.0),
        'cache_mem_v': nrm((DEPTH, DEC_BATCH, N_MEM, MEM_HEADS, MEM_HEAD_DIM), 1.0),
        'state_ssm_re': nrm((DEPTH, DEC_BATCH, N_GROUPS, SSM_STATE), 0.3),
        'state_ssm_im': nrm((DEPTH, DEC_BATCH, N_GROUPS, SSM_STATE), 0.3),
        'page_table': page_table,
        'mem_prompt': nrm((BATCH, N_MEM, D_MODEL), 1.0),
        'w_in': w_in,
        'ssm_a_re': a_re,
        'ssm_a_im': a_im,
        'ssm_log_dt': log_dt,
        'ssm_b_re': nrm((DEPTH, N_GROUPS, SSM_STATE, SSM_GROUP), (2 * SSM_GROUP) ** -0.5),
        'ssm_b_im': nrm((DEPTH, N_GROUPS, SSM_STATE, SSM_GROUP), (2 * SSM_GROUP) ** -0.5),
        'ssm_c_re': nrm((DEPTH, N_GROUPS, SSM_GROUP, SSM_STATE), SSM_STATE ** -0.5),
        'ssm_c_im': nrm((DEPTH, N_GROUPS, SSM_GROUP, SSM_STATE), SSM_STATE ** -0.5),
        'ssm_d': nrm((DEPTH, N_GROUPS, SSM_GROUP), 1.0),
        'w_glu_val': nrm((DEPTH, D_SSM, D_MODEL), D_SSM ** -0.5 * DN_BETA),
        'w_glu_gate': nrm((DEPTH, D_SSM, D_MODEL), D_SSM ** -0.5),
        'sb_bias': SB_BIAS_INIT + nrm((DEPTH, SB_HEADS), 0.1),
        'w_sb_branch': nrm((DEPTH, D_SB, D_MODEL), D_SB ** -0.5 * DN_BETA),
        'w_mix_out': nrm((DEPTH, D_MODEL, D_MODEL), sd * DN_BETA),
        'ln1_g': ln1_g, 'ln1_b': ln1_b,
        'w_mem_q': nrm((DEPTH, D_MODEL, D_MODEL), sd),
        'w_mem_k': nrm((DEPTH, D_MODEL, D_MODEL), sd),
        'w_mem_v': nrm((DEPTH, D_MODEL, D_MODEL), sd * DN_BETA),
        'w_mem_o': nrm((DEPTH, D_MODEL, D_MODEL), sd * DN_BETA),
        'ln2_g': ln2_g, 'ln2_b': ln2_b,
        'w_ff1': nrm((DEPTH, D_MODEL, D_FF), sd * DN_BETA),
        'w_ff2': nrm((DEPTH, D_FF, D_MODEL), D_FF ** -0.5 * DN_BETA),
        'ln3_g': ln3_g, 'ln3_b': ln3_b,
    }


def reference(x_prompt, x_sample, cache_k, cache_v, cache_mem_k, cache_mem_v, state_ssm_re, state_ssm_im,
              page_table, mem_prompt, w_in, ssm_a_re, ssm_a_im, ssm_log_dt, ssm_b_re, ssm_b_im, ssm_c_re,
              ssm_c_im, ssm_d, w_glu_val, w_glu_gate, sb_bias, w_sb_branch, w_mix_out, ln1_g, ln1_b,
              w_mem_q, w_mem_k, w_mem_v, w_mem_o, ln2_g, ln2_b, w_ff1, w_ff2, ln3_g, ln3_b):
    xp, xs = x_prompt, x_sample
    kp_l, vp_l, hrp_l, hip_l, mkp_l, mvp_l = [], [], [], [], [], []
    ks_l, vs_l, hrs_l, his_l = [], [], [], []
    for l in range(DEPTH):
        disc = s5_discretise(ssm_a_re[l], ssm_a_im[l], ssm_log_dt[l], ssm_b_re[l], ssm_b_im[l])
        sb_full = functools.partial(sb_prompt, bias=sb_bias[l])
        xp, kp, vp, hrp, hip = mixing_sublayer(
            xp, w_in[l], disc, ssm_c_re[l], ssm_c_im[l], ssm_d[l], w_glu_val[l], w_glu_gate[l],
            w_sb_branch[l], w_mix_out[l], ln1_g[l], ln1_b[l], sb_full, None, None)
        mkp, mvp = memory_kv(mem_prompt, w_mem_k[l], w_mem_v[l])
        xp = post_norm(xp, cross_attention(xp, mkp, mvp, w_mem_q[l], w_mem_o[l]), ln2_g[l], ln2_b[l])
        xp = post_norm(xp, sq_relu_mlp(xp, w_ff1[l], w_ff2[l]), ln3_g[l], ln3_b[l])
        sb_paged = functools.partial(sb_sample, bias=sb_bias[l], pool_k=cache_k, pool_v=cache_v,
                                     layer=l, page_table=page_table)
        xs, ks_, vs_, hrs, his = mixing_sublayer(
            xs, w_in[l], disc, ssm_c_re[l], ssm_c_im[l], ssm_d[l], w_glu_val[l], w_glu_gate[l],
            w_sb_branch[l], w_mix_out[l], ln1_g[l], ln1_b[l], sb_paged, state_ssm_re[l], state_ssm_im[l])
        xs = post_norm(xs, cross_attention(xs, cache_mem_k[l], cache_mem_v[l], w_mem_q[l], w_mem_o[l]),
                       ln2_g[l], ln2_b[l])
        xs = post_norm(xs, sq_relu_mlp(xs, w_ff1[l], w_ff2[l]), ln3_g[l], ln3_b[l])
        kp_l.append(kp); vp_l.append(vp); hrp_l.append(hrp); hip_l.append(hip)
        mkp_l.append(mkp); mvp_l.append(mvp)
        ks_l.append(ks_); vs_l.append(vs_); hrs_l.append(hrs); his_l.append(his)
    return (xp, xs,
            jnp.stack(kp_l), jnp.stack(vp_l), jnp.stack(hrp_l), jnp.stack(hip_l),
            jnp.stack(mkp_l), jnp.stack(mvp_l),
            jnp.stack(ks_l), jnp.stack(vs_l), jnp.stack(hrs_l), jnp.stack(his_l))
```

```python
import functools
import math

import jax
import jax.numpy as jnp
from jax import lax
from jax.experimental import pallas as pl
from jax.experimental.pallas import tpu as pltpu

F32 = jnp.float32
BF16 = jnp.bfloat16

LN_EPS = 1e-5
SSM_GROUP = 16
SSM_STATE = 64
SB_HEAD_DIM = 128
MEM_HEADS = 4

LANES = 128
SUBLANES = 8
VMEM_LIMIT_BYTES = 56 * 1024 * 1024

SCAN_SLABS = 8
SCAN_BLOCK_COLS = SCAN_SLABS * LANES
SCAN_BLOCK_CH = SCAN_BLOCK_COLS // SSM_STATE * SSM_GROUP
SCAN_CHUNK = 256
SCAN_PITCH = SCAN_CHUNK + SUBLANES


def _cparams(semantics):
    return pltpu.CompilerParams(dimension_semantics=semantics, vmem_limit_bytes=VMEM_LIMIT_BYTES)


def _mm_body(*refs, n_b, n_extra, n_out, nk, epilogue):
    a_ref = refs[0]
    b_refs = refs[1:1 + n_b]
    e_refs = refs[1 + n_b:1 + n_b + n_extra]
    o_refs = refs[1 + n_b + n_extra:1 + n_b + n_extra + n_out]
    acc_refs = refs[1 + n_b + n_extra + n_out:]
    a = a_ref[...]
    parts = [jnp.dot(a, b[...].astype(BF16), preferred_element_type=F32) for b in b_refs]

    def finish(accs):
        outs = epilogue(accs, [e[...] for e in e_refs])
        for o_ref, o in zip(o_refs, outs):
            o_ref[...] = o.astype(o_ref.dtype)

    if nk == 1:
        finish(parts)
    else:
        k = pl.program_id(2)

        @pl.when(k == 0)
        def _():
            for acc_ref, p in zip(acc_refs, parts):
                acc_ref[...] = p

        @pl.when(k > 0)
        def _():
            for acc_ref, p in zip(acc_refs, parts):
                acc_ref[...] += p

        @pl.when(k == nk - 1)
        def _():
            finish([acc_ref[...] for acc_ref in acc_refs])


def _matmul(a, bs, *, n, tm, tn, tk=None, b_col=0, epilogue, extras=(), out_dtypes):
    m, k_dim = a.shape
    tk = k_dim if tk is None else tk
    assert m % tm == 0 and n % tn == 0 and k_dim % tk == 0 and b_col % tn == 0
    nk = k_dim // tk
    boff = b_col // tn
    in_specs = [pl.BlockSpec((tm, tk), lambda i, j, k: (i, k))]
    for _ in bs:
        in_specs.append(pl.BlockSpec((tk, tn), lambda i, j, k: (k, j + boff)))
    operands = [a, *bs]
    for arr, kind, col in extras:
        assert col % tn == 0
        eoff = col // tn
        if kind == 'tile':
            in_specs.append(pl.BlockSpec((tm, tn), lambda i, j, k, eoff=eoff: (i, j + eoff)))
        else:
            in_specs.append(pl.BlockSpec((1, tn), lambda i, j, k, eoff=eoff: (0, j + eoff)))
        operands.append(arr)
    out_shape = [jax.ShapeDtypeStruct((m, n), dt) for dt in out_dtypes]
    out_specs = [pl.BlockSpec((tm, tn), lambda i, j, k: (i, j)) for _ in out_dtypes]
    scratch = [pltpu.VMEM((tm, tn), F32) for _ in bs] if nk > 1 else []
    body = functools.partial(_mm_body, n_b=len(bs), n_extra=len(extras), n_out=len(out_dtypes),
                             nk=nk, epilogue=epilogue)
    outs = pl.pallas_call(
        body,
        grid=(m // tm, n // tn, nk),
        in_specs=in_specs,
        out_specs=out_specs,
        out_shape=out_shape,
        scratch_shapes=scratch,
        compiler_params=_cparams(("parallel", "parallel", "arbitrary")),
    )(*operands)
    return outs


def _epi_plain(accs, extras):
    return [accs[0]]


def _epi_two_copies(accs, extras):
    return [accs[0], accs[0]]


def _epi_sigmoid(accs, extras):
    return [jax.nn.sigmoid(accs[0])]


def _epi_glu(accs, extras):
    return [extras[0] * (accs[0] * jax.nn.sigmoid(accs[1]))]


def _epi_merge(accs, extras):
    return [extras[0] + extras[1] * accs[0]]


def _epi_residual(accs, extras, *, alpha):
    return [alpha * extras[0] + accs[0]]


def _epi_relu2(accs, extras):
    return [jnp.square(jnp.maximum(accs[0], 0.0))]


def _ln_body(s_ref, g_ref, b_ref, *o_refs):
    x = s_ref[...]
    mu = jnp.mean(x, axis=-1, keepdims=True)
    xc = x - mu
    var = jnp.mean(xc * xc, axis=-1, keepdims=True)
    y = xc * lax.rsqrt(var + LN_EPS) * g_ref[...] + b_ref[...]
    for o_ref in o_refs:
        o_ref[...] = y.astype(o_ref.dtype)


def _layer_norm(s, g, b, *, tm, out_dtypes):
    m, d = s.shape
    assert m % tm == 0
    row = pl.BlockSpec((tm, d), lambda i: (i, 0))
    vec = pl.BlockSpec((1, d), lambda i: (0, 0))
    return pl.pallas_call(
        _ln_body,
        grid=(m // tm,),
        in_specs=[row, vec, vec],
        out_specs=[row for _ in out_dtypes],
        out_shape=[jax.ShapeDtypeStruct((m, d), dt) for dt in out_dtypes],
        compiler_params=_cparams(("parallel",)),
    )(s, g.reshape(1, d).astype(F32), b.reshape(1, d).astype(F32))


def _s5_discretise(a_re, a_im, log_dt, b_re, b_im):
    a_re, a_im = a_re.astype(F32), a_im.astype(F32)
    b_re, b_im = b_re.astype(F32), b_im.astype(F32)
    dt = jnp.exp(log_dt.astype(F32))[:, None]
    ld_re, ld_im = a_re * dt, a_im * dt
    mag = jnp.exp(ld_re)
    lb_re, lb_im = mag * jnp.cos(ld_im), mag * jnp.sin(ld_im)
    den = a_re * a_re + a_im * a_im
    f_re = ((lb_re - 1.0) * a_re + lb_im * a_im) / den
    f_im = (lb_im * a_re - (lb_re - 1.0) * a_im) / den
    bb_re = f_re[..., None] * b_re - f_im[..., None] * b_im
    bb_im = f_re[..., None] * b_im + f_im[..., None] * b_re
    return lb_re, lb_im, bb_re, bb_im


def _s5_layout(lb_re, lb_im, bb_re, bb_im, c_re, c_im, d_skip):
    g, p, c = bb_re.shape
    nblk = g * p // SCAN_BLOCK_COLS
    gpb = g // nblk
    eye = jnp.eye(gpb, dtype=F32)

    def expand_b(bb):
        bb4 = bb.reshape(nblk, gpb, p, c)
        return jnp.einsum('jgpc,gh->jgchp', bb4, eye).reshape(nblk, gpb * c, gpb * p)

    def expand_c(cc):
        c4 = cc.astype(F32).reshape(nblk, gpb, c, p)
        return jnp.einsum('jgcp,gh->jgphc', c4, eye).reshape(nblk, gpb * p, gpb * c)

    bd = jnp.concatenate([expand_b(bb_re), expand_b(bb_im)], axis=2).astype(BF16)
    cd = jnp.concatenate([expand_c(c_re), -expand_c(c_im)], axis=1).astype(BF16)

    def scan_layout(lb):
        return lb.reshape(nblk, SCAN_SLABS, LANES).transpose(1, 0, 2)

    lam_scan = jnp.stack([scan_layout(lb_re), scan_layout(lb_im)])
    lam_row = jnp.stack([lb_re.reshape(1, g * p), lb_im.reshape(1, g * p)])
    d_row = d_skip.astype(F32).reshape(1, g * c)
    return bd, cd, lam_scan, lam_row, d_row


def _s5_prompt_body(u_ref, bd_ref, cd_ref, lam_ref, d_ref, y_ref, hre_ref, him_ref, s_re, s_im, hc):
    ti = pl.program_id(1)
    nblk = bd_ref.shape[0]
    tc = u_ref.shape[0]
    half = SCAN_BLOCK_COLS

    @pl.when(ti == 0)
    def _():
        hc[...] = jnp.zeros_like(hc)

    ub = u_ref[...].astype(BF16)
    for j in range(nblk):
        bu = jnp.dot(ub[:, j * SCAN_BLOCK_CH:(j + 1) * SCAN_BLOCK_CH], bd_ref[j], preferred_element_type=F32)
        for k in range(SCAN_SLABS):
            s_re[k, j * SCAN_PITCH:j * SCAN_PITCH + tc, :] = bu[:, k * LANES:(k + 1) * LANES]
            s_im[k, j * SCAN_PITCH:j * SCAN_PITCH + tc, :] = bu[:, half + k * LANES:half + (k + 1) * LANES]

    lr = [lam_ref[0, k] for k in range(SCAN_SLABS)]
    li = [lam_ref[1, k] for k in range(SCAN_SLABS)]

    def step(t, carry):
        hr, hi = carry
        new_r, new_i = [], []
        for k in range(SCAN_SLABS):
            rows = pl.ds(t, SUBLANES, stride=SCAN_PITCH)
            nr = lr[k] * hr[k] - li[k] * hi[k] + s_re[k, rows, :]
            ni = lr[k] * hi[k] + li[k] * hr[k] + s_im[k, rows, :]
            s_re[k, rows, :] = nr
            s_im[k, rows, :] = ni
            new_r.append(nr)
            new_i.append(ni)
        return tuple(new_r), tuple(new_i)

    init = (tuple(hc[0, k] for k in range(SCAN_SLABS)), tuple(hc[1, k] for k in range(SCAN_SLABS)))
    hr, hi = lax.fori_loop(0, tc, step, init)
    for k in range(SCAN_SLABS):
        hc[0, k] = hr[k]
        hc[1, k] = hi[k]

    for j in range(nblk):
        rows = slice(j * SCAN_PITCH, j * SCAN_PITCH + tc)
        hcat = jnp.concatenate([s_re[k, rows, :] for k in range(SCAN_SLABS)]
                               + [s_im[k, rows, :] for k in range(SCAN_SLABS)], axis=-1).astype(BF16)
        cols = slice(j * SCAN_BLOCK_CH, (j + 1) * SCAN_BLOCK_CH)
        y = jnp.dot(hcat, cd_ref[j], preferred_element_type=F32) + d_ref[:, cols] * u_ref[:, cols]
        y_ref[:, cols] = jax.nn.gelu(y).astype(y_ref.dtype)

    @pl.when(ti == pl.num_programs(1) - 1)
    def _():
        hre_ref[0] = hc[0]
        him_ref[0] = hc[1]


def _s5_prompt(u, n_seq, bd, cd, lam_scan, d_row):
    m, d_ssm = u.shape
    t = m // n_seq
    tc = SCAN_CHUNK
    assert t % tc == 0
    nblk = bd.shape[0]
    nt = t // tc
    const3 = lambda shape: pl.BlockSpec(shape, lambda n, i: (0, 0, 0), pipeline_mode=pl.Buffered(1))
    state_shape = jax.ShapeDtypeStruct((n_seq, SCAN_SLABS, SUBLANES, LANES), F32)
    state_spec = pl.BlockSpec((1, SCAN_SLABS, SUBLANES, LANES), lambda n, i: (n, 0, 0, 0))
    y, hre, him = pl.pallas_call(
        _s5_prompt_body,
        grid=(n_seq, nt),
        in_specs=[
            pl.BlockSpec((tc, d_ssm), lambda n, i: (n * nt + i, 0)),
            const3(bd.shape),
            const3(cd.shape),
            pl.BlockSpec(lam_scan.shape, lambda n, i: (0, 0, 0, 0), pipeline_mode=pl.Buffered(1)),
            pl.BlockSpec(d_row.shape, lambda n, i: (0, 0)),
        ],
        out_specs=[pl.BlockSpec((tc, d_ssm), lambda n, i: (n * nt + i, 0)), state_spec, state_spec],
        out_shape=[jax.ShapeDtypeStruct((m, d_ssm), BF16), state_shape, state_shape],
        scratch_shapes=[
            pltpu.VMEM((SCAN_SLABS, nblk * SCAN_PITCH, LANES), F32),
            pltpu.VMEM((SCAN_SLABS, nblk * SCAN_PITCH, LANES), F32),
            pltpu.VMEM((2, SCAN_SLABS, SUBLANES, LANES), F32),
        ],
        compiler_params=_cparams(("arbitrary", "arbitrary")),
    )(u, bd, cd, lam_scan, d_row)

    def to_gp(h):
        return h.transpose(0, 2, 1, 3).reshape(n_seq, -1, SSM_STATE)

    return y, to_gp(hre), to_gp(him)


def _s5_sample_body(u_ref, bd_ref, cd_ref, lam_ref, d_ref, h0re_ref, h0im_ref, y_ref, hre_ref, him_ref):
    half = SCAN_BLOCK_COLS
    u = u_ref[...]
    bu = jnp.dot(u.astype(BF16), bd_ref[0], preferred_element_type=F32)
    lr, li = lam_ref[0], lam_ref[1]
    h0r, h0i = h0re_ref[...], h0im_ref[...]
    hr = bu[:, :half] + (lr * h0r - li * h0i)
    hi = bu[:, half:] + (lr * h0i + li * h0r)
    hre_ref[...] = hr
    him_ref[...] = hi
    hcat = jnp.concatenate([hr, hi], axis=-1).astype(BF16)
    y = jnp.dot(hcat, cd_ref[0], preferred_element_type=F32) + d_ref[...] * u
    y_ref[...] = jax.nn.gelu(y).astype(y_ref.dtype)


def _s5_sample(u, h0_re, h0_im, bd, cd, lam_row, d_row):
    b, d_ssm = u.shape
    nblk = bd.shape[0]
    cols = SCAN_BLOCK_COLS
    ch = SCAN_BLOCK_CH
    state = pl.BlockSpec((b, cols), lambda j: (0, j))
    return pl.pallas_call(
        _s5_sample_body,
        grid=(nblk,),
        in_specs=[
            pl.BlockSpec((b, ch), lambda j: (0, j)),
            pl.BlockSpec((1,) + bd.shape[1:], lambda j: (j, 0, 0)),
            pl.BlockSpec((1,) + cd.shape[1:], lambda j: (j, 0, 0)),
            pl.BlockSpec((2, 1, cols), lambda j: (0, 0, j)),
            pl.BlockSpec((1, ch), lambda j: (0, j)),
            state, state,
        ],
        out_specs=[pl.BlockSpec((b, ch), lambda j: (0, j)), state, state],
        out_shape=[jax.ShapeDtypeStruct((b, d_ssm), BF16),
                   jax.ShapeDtypeStruct(h0_re.shape, F32), jax.ShapeDtypeStruct(h0_im.shape, F32)],
        compiler_params=_cparams(("parallel",)),
    )(u, bd, cd, lam_row, d_row, h0_re, h0_im)


def _log_keep(z):
    return -(jnp.maximum(z, 0.0) + jnp.log1p(jnp.exp(-jnp.abs(z))))


def _strict_lower_ones(n):
    r = lax.broadcasted_iota(jnp.int32, (n, n), 0)
    c = lax.broadcasted_iota(jnp.int32, (n, n), 1)
    return r > c


def _suffix_after(lk, tri2):
    hi = lk.astype(BF16)
    lo = (lk - hi.astype(F32)).astype(BF16)
    return jnp.dot(jnp.concatenate([hi, lo], axis=-1), tri2, preferred_element_type=F32)


def _sb_prompt_body(bias_ref, q_ref, k_ref, v_ref, o_ref, *, scale):
    h = pl.program_id(1)
    qi = pl.program_id(2)
    tq = q_ref.shape[0]
    bias = bias_ref[h]
    q = q_ref[...]
    below = _strict_lower_ones(tq)
    tri = jnp.where(below, 1.0, 0.0).astype(BF16)
    tri2 = jnp.concatenate([tri, tri], axis=0)

    def block(kb, acc, run, diagonal):
        rows = pl.ds(pl.multiple_of(kb * tq, tq), tq)
        kblk = k_ref[rows, :].astype(BF16)
        vblk = v_ref[rows, :].astype(BF16)
        z = lax.dot_general(q, kblk, (((1,), (1,)), ((), ())), preferred_element_type=F32) * scale + bias
        lk = _log_keep(z)
        log_beta = z + lk
        if diagonal:
            lk = jnp.where(below, lk, 0.0)
        w = jnp.exp(log_beta + _suffix_after(lk, tri2) + run)
        if diagonal:
            w = jnp.where(below, w, 0.0)
        acc = acc + jnp.dot(w.astype(BF16), vblk, preferred_element_type=F32)
        run = run + jnp.sum(lk, axis=-1, keepdims=True)
        return acc, run

    acc, run = block(qi, jnp.zeros((tq, q_ref.shape[1]), F32), jnp.zeros((tq, 1), F32), True)
    acc, run = lax.fori_loop(0, qi, lambda i, c: block(qi - 1 - i, c[0], c[1], False), (acc, run))
    o_ref[...] = acc.astype(o_ref.dtype)


def _sb_prompt(q, k, v, bias, n_seq, *, tq):
    m, hd = q.shape
    t = m // n_seq
    heads = hd // SB_HEAD_DIM
    assert t % tq == 0
    nq = t // tq
    kv_spec = pl.BlockSpec((t, SB_HEAD_DIM), lambda n, h, i: (n, h))
    q_spec = pl.BlockSpec((tq, SB_HEAD_DIM), lambda n, h, i: (n * nq + i, h))
    return pl.pallas_call(
        functools.partial(_sb_prompt_body, scale=SB_HEAD_DIM ** -0.5),
        grid=(n_seq, heads, nq),
        in_specs=[pl.BlockSpec(memory_space=pltpu.SMEM), q_spec, kv_spec, kv_spec],
        out_specs=q_spec,
        out_shape=jax.ShapeDtypeStruct((m, hd), BF16),
        compiler_params=_cparams(("parallel", "parallel", "arbitrary")),
    )(bias.astype(F32), q, k, v)


def _head_mask(rows, cols, head_dim):
    r = lax.broadcasted_iota(jnp.int32, (rows, cols), 0)
    c = lax.broadcasted_iota(jnp.int32, (rows, cols), 1)
    return (c >= r * head_dim) & (c < (r + 1) * head_dim)


def _sb_sample_body(pt_ref, bias_ref, q_ref, knew_ref, vnew_ref, k_ref, v_ref, o_ref, qblk, acc, run, *,
                    scale, heads, new_key_visible):
    p = pl.program_id(1)
    hd = q_ref.shape[-1]
    page = k_ref.shape[0]
    hmask = _head_mask(heads, hd, SB_HEAD_DIM)
    bias = bias_ref[...]

    @pl.when(p == 0)
    def _():
        qrow = q_ref[...].astype(F32)
        qb = jnp.where(hmask, jnp.broadcast_to(qrow, (heads, hd)), 0.0)
        qblk[...] = qb.astype(BF16)
        z_new = jnp.sum(qb * knew_ref[...], axis=-1, keepdims=True) * scale + bias
        lk_new = _log_keep(z_new)
        w_new = jnp.exp(z_new + lk_new)
        if not new_key_visible:
            lk_new = jnp.zeros_like(lk_new)
            w_new = jnp.zeros_like(w_new)
        run[...] = lk_new
        acc[...] = w_new * jnp.broadcast_to(vnew_ref[...], (heads, hd))

    kp = k_ref[...].astype(BF16)
    vp = v_ref[...].astype(BF16)
    z = lax.dot_general(qblk[...], kp, (((1,), (1,)), ((), ())), preferred_element_type=F32) * scale + bias
    lk = _log_keep(z)
    tri = jnp.where(_strict_lower_ones(page), 1.0, 0.0).astype(BF16)
    tri2 = jnp.concatenate([tri, tri], axis=0)
    w = jnp.exp(z + lk + _suffix_after(lk, tri2) + run[...])
    acc[...] += jnp.dot(w.astype(BF16), vp, preferred_element_type=F32)
    run[...] += jnp.sum(lk, axis=-1, keepdims=True)

    @pl.when(p == pl.num_programs(1) - 1)
    def _():
        o_ref[...] = jnp.sum(jnp.where(hmask, acc[...], 0.0), axis=0, keepdims=True).astype(o_ref.dtype)


def _sb_sample(q, k_new, v_new, bias, pool_k, pool_v, page_table, layer):
    b, hd = q.shape
    heads = hd // SB_HEAD_DIM
    depth, n_pool, page = pool_k.shape[:3]
    n_pages = page_table.shape[1]
    past = n_pages * page
    new_key_pos, query_pos = past, past
    new_key_visible = new_key_pos < query_pos
    pk = pool_k.reshape(depth * n_pool, page, hd)
    pv = pool_v.reshape(depth * n_pool, page, hd)
    base = layer * n_pool
    row = pl.BlockSpec((None, 1, hd), lambda n, p, pt: (n, 0, 0))
    page_spec = pl.BlockSpec((None, page, hd), lambda n, p, pt: (base + pt[n * n_pages + n_pages - 1 - p], 0, 0))
    grid_spec = pltpu.PrefetchScalarGridSpec(
        num_scalar_prefetch=1,
        grid=(b, n_pages),
        in_specs=[pl.BlockSpec((heads, 1), lambda n, p, pt: (0, 0)), row, row, row, page_spec, page_spec],
        out_specs=row,
        scratch_shapes=[pltpu.VMEM((heads, hd), BF16), pltpu.VMEM((heads, hd), F32), pltpu.VMEM((heads, 1), F32)],
    )
    out = pl.pallas_call(
        functools.partial(_sb_sample_body, scale=SB_HEAD_DIM ** -0.5, heads=heads,
                          new_key_visible=new_key_visible),
        grid_spec=grid_spec,
        out_shape=jax.ShapeDtypeStruct((b, 1, hd), BF16),
        compiler_params=_cparams(("parallel", "arbitrary")),
    )(page_table.reshape(-1).astype(jnp.int32), bias.astype(F32).reshape(heads, 1),
      q.reshape(b, 1, hd), k_new.reshape(b, 1, hd), v_new.reshape(b, 1, hd), pk, pv)
    return out.reshape(b, hd)


def _xattn_prompt_body(q_ref, k_ref, v_ref, o_ref, *, scale):
    q = q_ref[...]
    kb = k_ref[...].astype(BF16)
    vb = v_ref[...].astype(BF16)
    s = lax.dot_general(q, kb, (((1,), (1,)), ((), ())), preferred_element_type=F32) * scale
    s = s - jnp.max(s, axis=-1, keepdims=True)
    e = jnp.exp(s)
    p = e / jnp.sum(e, axis=-1, keepdims=True)
    o_ref[...] = jnp.dot(p.astype(BF16), vb, preferred_element_type=F32).astype(o_ref.dtype)


def _xattn_prompt(q, mem_k, mem_v, n_seq, *, tq):
    m, d = q.shape
    t = m // n_seq
    hd = d // MEM_HEADS
    n_mem = mem_k.shape[0] // n_seq
    nq = t // tq
    q_spec = pl.BlockSpec((tq, hd), lambda n, h, i: (n * nq + i, h))
    kv_spec = pl.BlockSpec((n_mem, hd), lambda n, h, i: (n, h))
    return pl.pallas_call(
        functools.partial(_xattn_prompt_body, scale=hd ** -0.5),
        grid=(n_seq, MEM_HEADS, nq),
        in_specs=[q_spec, kv_spec, kv_spec],
        out_specs=q_spec,
        out_shape=jax.ShapeDtypeStruct((m, d), BF16),
        compiler_params=_cparams(("parallel", "parallel", "parallel")),
    )(q, mem_k, mem_v)


def _xattn_sample_body(q_ref, k_ref, v_ref, o_ref, *, scale, head_dim):
    d = q_ref.shape[-1]
    hmask = _head_mask(SUBLANES, d, head_dim)
    qb = jnp.where(hmask, jnp.broadcast_to(q_ref[...].astype(F32), (SUBLANES, d)), 0.0).astype(BF16)
    kb = k_ref[...].astype(BF16)
    vb = v_ref[...].astype(BF16)
    s = lax.dot_general(qb, kb, (((1,), (1,)), ((), ())), preferred_element_type=F32) * scale
    s = s - jnp.max(s, axis=-1, keepdims=True)
    e = jnp.exp(s)
    p = e / jnp.sum(e, axis=-1, keepdims=True)
    o = jnp.dot(p.astype(BF16), vb, preferred_element_type=F32)
    o_ref[...] = jnp.sum(jnp.where(hmask, o, 0.0), axis=0, keepdims=True).astype(o_ref.dtype)


def _xattn_sample(q, mem_k, mem_v):
    b, d = q.shape
    n_mem = mem_k.shape[1]
    row = pl.BlockSpec((None, 1, d), lambda n: (n, 0, 0))
    kv = pl.BlockSpec((None, n_mem, d), lambda n: (n, 0, 0))
    out = pl.pallas_call(
        functools.partial(_xattn_sample_body, scale=(d // MEM_HEADS) ** -0.5, head_dim=d // MEM_HEADS),
        grid=(b,),
        in_specs=[row, kv, kv],
        out_specs=row,
        out_shape=jax.ShapeDtypeStruct((b, 1, d), BF16),
        compiler_params=_cparams(("parallel",)),
    )(q.reshape(b, 1, d), mem_k, mem_v)
    return out.reshape(b, d)


def _layer(x, w, *, tm, tn, ff2_tiles, sb_fn, s5_fn, xattn_fn, alpha):
    m, d = x.shape
    d_ssm = w['w_glu_val'].shape[0]
    d_sb = w['w_sb_branch'].shape[0]
    mm = functools.partial(_matmul, tm=tm, tn=tn)
    xb = x.astype(BF16)
    w_in = w['w_in']
    u, = mm(xb, [w_in], n=d_ssm, b_col=0, epilogue=_epi_plain, out_dtypes=[F32])
    q, = mm(xb, [w_in], n=d_sb, b_col=d_ssm, epilogue=_epi_plain, out_dtypes=[BF16])
    k, = mm(xb, [w_in], n=d_sb, b_col=d_ssm + d_sb, epilogue=_epi_plain, out_dtypes=[F32])
    v, = mm(xb, [w_in], n=d_sb, b_col=d_ssm + 2 * d_sb, epilogue=_epi_plain, out_dtypes=[F32])
    gates, = mm(xb, [w_in], n=2 * d, b_col=d_ssm + 3 * d_sb, epilogue=_epi_sigmoid, out_dtypes=[F32])

    y, s5_state = s5_fn(u)
    ssm, = mm(y, [w['w_glu_val'], w['w_glu_gate']], n=d, epilogue=_epi_glu,
              extras=[(gates, 'tile', 0)], out_dtypes=[F32])
    sb = sb_fn(q, k, v)
    merged, = mm(sb, [w['w_sb_branch']], n=d, epilogue=_epi_merge,
                 extras=[(ssm, 'tile', 0), (gates, 'tile', d)], out_dtypes=[BF16])
    residual = functools.partial(_epi_residual, alpha=alpha)
    s1, = mm(merged, [w['w_mix_out']], n=d, epilogue=residual, extras=[(x, 'tile', 0)], out_dtypes=[F32])
    x1, x1b = _layer_norm(s1, w['ln1_g'], w['ln1_b'], tm=min(tm, 256), out_dtypes=[F32, BF16])

    qm, = mm(x1b, [w['w_mem_q']], n=d, epilogue=_epi_plain, out_dtypes=[BF16])
    o = xattn_fn(qm)
    s2, = mm(o, [w['w_mem_o']], n=d, epilogue=residual, extras=[(x1, 'tile', 0)], out_dtypes=[F32])
    x2, x2b = _layer_norm(s2, w['ln2_g'], w['ln2_b'], tm=min(tm, 256), out_dtypes=[F32, BF16])

    hmid, = mm(x2b, [w['w_ff1']], n=w['w_ff1'].shape[1], epilogue=_epi_relu2, out_dtypes=[BF16])
    s3, = _matmul(hmid, [w['w_ff2_bf16']], n=d, **ff2_tiles, epilogue=residual,
                  extras=[(x2, 'tile', 0)], out_dtypes=[F32])
    x3, = _layer_norm(s3, w['ln3_g'], w['ln3_b'], tm=min(tm, 256), out_dtypes=[F32])
    return x3, k, v, s5_state


def kernel(x_prompt, x_sample, cache_k, cache_v, cache_mem_k, cache_mem_v, state_ssm_re, state_ssm_im, page_table, mem_prompt, w_in, ssm_a_re, ssm_a_im, ssm_log_dt, ssm_b_re, ssm_b_im, ssm_c_re, ssm_c_im, ssm_d, w_glu_val, w_glu_gate, sb_bias, w_sb_branch, w_mix_out, ln1_g, ln1_b, w_mem_q, w_mem_k, w_mem_v, w_mem_o, ln2_g, ln2_b, w_ff1, w_ff2, ln3_g, ln3_b):
    depth = w_in.shape[0]
    n_seq, seq, d = x_prompt.shape
    b_dec, dec_seq, _ = x_sample.shape
    assert dec_seq == 1, "the decode group is one token per sequence"
    n_mem = mem_prompt.shape[1]
    alpha = (2 * depth) ** 0.25
    heads = sb_bias.shape[1]
    groups = ssm_a_re.shape[1]

    xp = x_prompt.reshape(n_seq * seq, d)
    xs = x_sample.reshape(b_dec, d)
    memb = mem_prompt.reshape(n_seq * n_mem, d).astype(BF16)
    outs = {name: [] for name in ('kp', 'vp', 'hrp', 'hip', 'mkp', 'mvp', 'ks', 'vs', 'hrs', 'his')}
    for l in range(depth):
        w = dict(w_in=w_in[l], w_glu_val=w_glu_val[l], w_glu_gate=w_glu_gate[l], w_sb_branch=w_sb_branch[l],
                 w_mix_out=w_mix_out[l], ln1_g=ln1_g[l], ln1_b=ln1_b[l], w_mem_q=w_mem_q[l], w_mem_o=w_mem_o[l],
                 ln2_g=ln2_g[l], ln2_b=ln2_b[l], w_ff1=w_ff1[l], w_ff2_bf16=w_ff2[l].astype(BF16),
                 ln3_g=ln3_g[l], ln3_b=ln3_b[l])
        lb_re, lb_im, bb_re, bb_im = _s5_discretise(ssm_a_re[l], ssm_a_im[l], ssm_log_dt[l], ssm_b_re[l], ssm_b_im[l])
        bd, cd, lam_scan, lam_row, d_row = _s5_layout(lb_re, lb_im, bb_re, bb_im, ssm_c_re[l], ssm_c_im[l], ssm_d[l])

        mkp, = _matmul(memb, [w_mem_k[l]], n=d, tm=n_seq * n_mem, tn=256, epilogue=_epi_plain, out_dtypes=[F32])
        mvp, = _matmul(memb, [w_mem_v[l]], n=d, tm=n_seq * n_mem, tn=256, epilogue=_epi_plain, out_dtypes=[F32])

        def s5_prompt(u):
            y, hre, him = _s5_prompt(u, n_seq, bd, cd, lam_scan, d_row)
            return y, (hre, him)

        xp, kp, vp, (hrp, hip) = _layer(
            xp, w, tm=1024, tn=256, ff2_tiles=dict(tm=1024, tn=512, tk=2048), alpha=alpha, s5_fn=s5_prompt,
            sb_fn=lambda q, k, v: _sb_prompt(q, k, v, sb_bias[l], n_seq, tq=256),
            xattn_fn=lambda qm: _xattn_prompt(qm, mkp, mvp, n_seq, tq=512))

        def s5_sample(u):
            y, hre, him = _s5_sample(u, state_ssm_re[l].reshape(b_dec, -1), state_ssm_im[l].reshape(b_dec, -1),
                                     bd, cd, lam_row, d_row)
            return y, (hre.reshape(b_dec, groups, -1), him.reshape(b_dec, groups, -1))

        xs, ks, vs, (hrs, his) = _layer(
            xs, w, tm=b_dec, tn=512, ff2_tiles=dict(tm=b_dec, tn=512), alpha=alpha, s5_fn=s5_sample,
            sb_fn=lambda q, k, v: _sb_sample(q, k, v, sb_bias[l], cache_k, cache_v, page_table, l),
            xattn_fn=lambda qm: _xattn_sample(qm, cache_mem_k[l].reshape(b_dec, n_mem, d),
                                              cache_mem_v[l].reshape(b_dec, n_mem, d)))

        outs['kp'].append(kp.reshape(n_seq, seq, heads, -1))
        outs['vp'].append(vp.reshape(n_seq, seq, heads, -1))
        outs['hrp'].append(hrp)
        outs['hip'].append(hip)
        outs['mkp'].append(mkp.reshape(n_seq, n_mem, MEM_HEADS, -1))
        outs['mvp'].append(mvp.reshape(n_seq, n_mem, MEM_HEADS, -1))
        outs['ks'].append(ks.reshape(b_dec, dec_seq, heads, -1))
        outs['vs'].append(vs.reshape(b_dec, dec_seq, heads, -1))
        outs['hrs'].append(hrs)
        outs['his'].append(his)

    st = {name: jnp.stack(vals) for name, vals in outs.items()}
    return (xp.reshape(n_seq, seq, d), xs.reshape(b_dec, dec_seq, d),
            st['kp'], st['vp'], st['hrp'], st['hip'], st['mkp'], st['mvp'],
            st['ks'], st['vs'], st['hrs'], st['his'])
```

```python
import functools

import jax
import jax.numpy as jnp
from jax import lax
from jax.experimental import pallas as pl
from jax.experimental.pallas import tpu as pltpu

F32 = jnp.float32
BF16 = jnp.bfloat16

LN_EPS = 1e-5
SSM_GROUP = 16
SSM_STATE = 64
SB_HEAD_DIM = 128
MEM_HEADS = 4

LANES = 128
SUBLANES = 8
MXU_DIM = 256
VMEM_LIMIT_BYTES = 56 * 1024 * 1024

SCAN_SLABS = 8
SCAN_BLOCK_COLS = SCAN_SLABS * LANES
SCAN_BLOCK_CH = SCAN_BLOCK_COLS // SSM_STATE * SSM_GROUP
SCAN_CHUNK = 256
SCAN_PITCH = SCAN_CHUNK + SUBLANES

SB_PROMPT_HEADS_PER_STEP = 4
SB_SAMPLE_PAGES_PER_STEP = 4


def _cparams(semantics):
    return pltpu.CompilerParams(dimension_semantics=semantics, vmem_limit_bytes=VMEM_LIMIT_BYTES)


def _mm_body(*refs, n_b, n_extra, n_out, nk, epilogue):
    a_ref = refs[0]
    b_refs = refs[1:1 + n_b]
    e_refs = refs[1 + n_b:1 + n_b + n_extra]
    o_refs = refs[1 + n_b + n_extra:1 + n_b + n_extra + n_out]
    acc_refs = refs[1 + n_b + n_extra + n_out:]
    a = a_ref[...]
    parts = [jnp.dot(a, b[...].astype(BF16), preferred_element_type=F32) for b in b_refs]

    def finish(accs):
        outs = epilogue(accs, [e[...] for e in e_refs])
        for o_ref, o in zip(o_refs, outs):
            o_ref[...] = o.astype(o_ref.dtype)

    if nk == 1:
        finish(parts)
    else:
        k = pl.program_id(2)

        @pl.when(k == 0)
        def _():
            for acc_ref, p in zip(acc_refs, parts):
                acc_ref[...] = p

        @pl.when(k > 0)
        def _():
            for acc_ref, p in zip(acc_refs, parts):
                acc_ref[...] += p

        @pl.when(k == nk - 1)
        def _():
            finish([acc_ref[...] for acc_ref in acc_refs])


def _matmul(a, bs, *, name, n, tm, tn, tk=None, b_col=0, epilogue, extras=(), out_dtypes):
    m, k_dim = a.shape
    tk = k_dim if tk is None else tk
    assert m % tm == 0 and n % tn == 0 and k_dim % tk == 0 and b_col % tn == 0
    nk = k_dim // tk
    boff = b_col // tn
    in_specs = [pl.BlockSpec((tm, tk), lambda i, j, k: (i, k))]
    for _ in bs:
        in_specs.append(pl.BlockSpec((tk, tn), lambda i, j, k: (k, j + boff)))
    operands = [a, *bs]
    for arr, kind, col in extras:
        assert col % tn == 0
        eoff = col // tn
        if kind == 'tile':
            in_specs.append(pl.BlockSpec((tm, tn), lambda i, j, k, eoff=eoff: (i, j + eoff)))
        else:
            in_specs.append(pl.BlockSpec((1, tn), lambda i, j, k, eoff=eoff: (0, j + eoff)))
        operands.append(arr)
    out_shape = [jax.ShapeDtypeStruct((m, n), dt) for dt in out_dtypes]
    out_specs = [pl.BlockSpec((tm, tn), lambda i, j, k: (i, j)) for _ in out_dtypes]
    scratch = [pltpu.VMEM((tm, tn), F32) for _ in bs] if nk > 1 else []
    body = functools.partial(_mm_body, n_b=len(bs), n_extra=len(extras), n_out=len(out_dtypes),
                             nk=nk, epilogue=epilogue)
    outs = pl.pallas_call(
        body,
        name=name,
        grid=(m // tm, n // tn, nk),
        in_specs=in_specs,
        out_specs=out_specs,
        out_shape=out_shape,
        scratch_shapes=scratch,
        compiler_params=_cparams(("parallel", "parallel", "arbitrary")),
    )(*operands)
    return outs


def _epi_plain(accs, extras):
    return [accs[0]]


def _epi_two_copies(accs, extras):
    return [accs[0], accs[0]]


def _epi_sigmoid(accs, extras):
    return [jax.nn.sigmoid(accs[0])]


def _epi_glu(accs, extras):
    return [extras[0] * (accs[0] * jax.nn.sigmoid(accs[1]))]


def _epi_merge(accs, extras):
    return [extras[0] + extras[1] * accs[0]]


def _epi_residual(accs, extras, *, alpha):
    return [alpha * extras[0] + accs[0]]


def _epi_relu2(accs, extras):
    return [jnp.square(jnp.maximum(accs[0], 0.0))]


def _ln_body(s_ref, g_ref, b_ref, *o_refs):
    x = s_ref[...]
    mu = jnp.mean(x, axis=-1, keepdims=True)
    xc = x - mu
    var = jnp.mean(xc * xc, axis=-1, keepdims=True)
    y = xc * lax.rsqrt(var + LN_EPS) * g_ref[...] + b_ref[...]
    for o_ref in o_refs:
        o_ref[...] = y.astype(o_ref.dtype)


def _layer_norm(s, g, b, *, name, tm, out_dtypes):
    m, d = s.shape
    assert m % tm == 0
    row = pl.BlockSpec((tm, d), lambda i: (i, 0))
    vec = pl.BlockSpec((1, d), lambda i: (0, 0))
    return pl.pallas_call(
        _ln_body,
        name=name,
        grid=(m // tm,),
        in_specs=[row, vec, vec],
        out_specs=[row for _ in out_dtypes],
        out_shape=[jax.ShapeDtypeStruct((m, d), dt) for dt in out_dtypes],
        compiler_params=_cparams(("parallel",)),
    )(s, g.reshape(1, d).astype(F32), b.reshape(1, d).astype(F32))


def _s5_discretise(a_re, a_im, log_dt, b_re, b_im):
    a_re, a_im = a_re.astype(F32), a_im.astype(F32)
    b_re, b_im = b_re.astype(F32), b_im.astype(F32)
    dt = jnp.exp(log_dt.astype(F32))[:, None]
    ld_re, ld_im = a_re * dt, a_im * dt
    mag = jnp.exp(ld_re)
    lb_re, lb_im = mag * jnp.cos(ld_im), mag * jnp.sin(ld_im)
    den = a_re * a_re + a_im * a_im
    f_re = ((lb_re - 1.0) * a_re + lb_im * a_im) / den
    f_im = (lb_im * a_re - (lb_re - 1.0) * a_im) / den
    bb_re = f_re[..., None] * b_re - f_im[..., None] * b_im
    bb_im = f_re[..., None] * b_im + f_im[..., None] * b_re
    return lb_re, lb_im, bb_re, bb_im


def _s5_layout(lb_re, lb_im, bb_re, bb_im, c_re, c_im, d_skip):
    g, p, c = bb_re.shape
    nblk = g * p // SCAN_BLOCK_COLS
    gpb = g // nblk
    eye = jnp.eye(gpb, dtype=F32)

    def expand_b(bb):
        bb4 = bb.reshape(nblk, gpb, p, c)
        return jnp.einsum('jgpc,gh->jgchp', bb4, eye).reshape(nblk, gpb * c, gpb * p)

    def expand_c(cc):
        c4 = cc.astype(F32).reshape(nblk, gpb, c, p)
        return jnp.einsum('jgcp,gh->jgphc', c4, eye).reshape(nblk, gpb * p, gpb * c)

    bd = jnp.concatenate([expand_b(bb_re), expand_b(bb_im)], axis=2).astype(BF16)
    cd = jnp.concatenate([expand_c(c_re), -expand_c(c_im)], axis=1).astype(BF16)

    def scan_layout(lb):
        return lb.reshape(nblk, SCAN_SLABS, LANES).transpose(1, 0, 2)

    lam_scan = jnp.stack([scan_layout(lb_re), scan_layout(lb_im)])
    lam_row = jnp.stack([lb_re.reshape(1, g * p), lb_im.reshape(1, g * p)])
    d_row = d_skip.astype(F32).reshape(1, g * c)
    return bd, cd, lam_scan, lam_row, d_row


def _s5_prompt_body(u_ref, bd_ref, cd_ref, lam_ref, d_ref, y_ref, hre_ref, him_ref, s_re, s_im, hc):
    ti = pl.program_id(1)
    nblk = bd_ref.shape[0]
    tc = u_ref.shape[0]
    half = SCAN_BLOCK_COLS

    @pl.when(ti == 0)
    def _():
        hc[...] = jnp.zeros_like(hc)

    ub = u_ref[...].astype(BF16)
    for j in range(nblk):
        bu = jnp.dot(ub[:, j * SCAN_BLOCK_CH:(j + 1) * SCAN_BLOCK_CH], bd_ref[j], preferred_element_type=F32)
        for k in range(SCAN_SLABS):
            s_re[k, j * SCAN_PITCH:j * SCAN_PITCH + tc, :] = bu[:, k * LANES:(k + 1) * LANES]
            s_im[k, j * SCAN_PITCH:j * SCAN_PITCH + tc, :] = bu[:, half + k * LANES:half + (k + 1) * LANES]

    lr = [lam_ref[0, k] for k in range(SCAN_SLABS)]
    li = [lam_ref[1, k] for k in range(SCAN_SLABS)]

    def step(t, carry):
        hr, hi = carry
        new_r, new_i = [], []
        for k in range(SCAN_SLABS):
            rows = pl.ds(t, SUBLANES, stride=SCAN_PITCH)
            nr = lr[k] * hr[k] - li[k] * hi[k] + s_re[k, rows, :]
            ni = lr[k] * hi[k] + li[k] * hr[k] + s_im[k, rows, :]
            s_re[k, rows, :] = nr
            s_im[k, rows, :] = ni
            new_r.append(nr)
            new_i.append(ni)
        return tuple(new_r), tuple(new_i)

    init = (tuple(hc[0, k] for k in range(SCAN_SLABS)), tuple(hc[1, k] for k in range(SCAN_SLABS)))
    hr, hi = lax.fori_loop(0, tc, step, init)
    for k in range(SCAN_SLABS):
        hc[0, k] = hr[k]
        hc[1, k] = hi[k]

    for j in range(nblk):
        rows = slice(j * SCAN_PITCH, j * SCAN_PITCH + tc)
        hcat = jnp.concatenate([s_re[k, rows, :] for k in range(SCAN_SLABS)]
                               + [s_im[k, rows, :] for k in range(SCAN_SLABS)], axis=-1).astype(BF16)
        cols = slice(j * SCAN_BLOCK_CH, (j + 1) * SCAN_BLOCK_CH)
        y = jnp.dot(hcat, cd_ref[j], preferred_element_type=F32) + d_ref[:, cols] * u_ref[:, cols]
        y_ref[:, cols] = jax.nn.gelu(y).astype(y_ref.dtype)

    @pl.when(ti == pl.num_programs(1) - 1)
    def _():
        hre_ref[0] = hc[0]
        him_ref[0] = hc[1]


def _s5_prompt(u, n_seq, bd, cd, lam_scan, d_row):
    m, d_ssm = u.shape
    t = m // n_seq
    tc = SCAN_CHUNK
    assert t % tc == 0
    nblk = bd.shape[0]
    nt = t // tc
    const3 = lambda shape: pl.BlockSpec(shape, lambda n, i: (0, 0, 0), pipeline_mode=pl.Buffered(1))
    state_shape = jax.ShapeDtypeStruct((n_seq, SCAN_SLABS, SUBLANES, LANES), F32)
    state_spec = pl.BlockSpec((1, SCAN_SLABS, SUBLANES, LANES), lambda n, i: (n, 0, 0, 0))
    y, hre, him = pl.pallas_call(
        _s5_prompt_body,
        name="s5_prompt",
        grid=(n_seq, nt),
        in_specs=[
            pl.BlockSpec((tc, d_ssm), lambda n, i: (n * nt + i, 0)),
            const3(bd.shape),
            const3(cd.shape),
            pl.BlockSpec(lam_scan.shape, lambda n, i: (0, 0, 0, 0), pipeline_mode=pl.Buffered(1)),
            pl.BlockSpec(d_row.shape, lambda n, i: (0, 0)),
        ],
        out_specs=[pl.BlockSpec((tc, d_ssm), lambda n, i: (n * nt + i, 0)), state_spec, state_spec],
        out_shape=[jax.ShapeDtypeStruct((m, d_ssm), BF16), state_shape, state_shape],
        scratch_shapes=[
            pltpu.VMEM((SCAN_SLABS, nblk * SCAN_PITCH, LANES), F32),
            pltpu.VMEM((SCAN_SLABS, nblk * SCAN_PITCH, LANES), F32),
            pltpu.VMEM((2, SCAN_SLABS, SUBLANES, LANES), F32),
        ],
        compiler_params=_cparams(("arbitrary", "arbitrary")),
    )(u, bd, cd, lam_scan, d_row)

    def to_gp(h):
        return h.transpose(0, 2, 1, 3).reshape(n_seq, -1, SSM_STATE)

    return y, to_gp(hre), to_gp(him)


def _s5_sample_body(u_ref, bd_ref, cd_ref, lam_ref, d_ref, h0re_ref, h0im_ref, y_ref, hre_ref, him_ref):
    half = SCAN_BLOCK_COLS
    u = u_ref[...]
    bu = jnp.dot(u.astype(BF16), bd_ref[0], preferred_element_type=F32)
    lr, li = lam_ref[0], lam_ref[1]
    h0r, h0i = h0re_ref[...], h0im_ref[...]
    hr = bu[:, :half] + (lr * h0r - li * h0i)
    hi = bu[:, half:] + (lr * h0i + li * h0r)
    hre_ref[...] = hr
    him_ref[...] = hi
    hcat = jnp.concatenate([hr, hi], axis=-1).astype(BF16)
    y = jnp.dot(hcat, cd_ref[0], preferred_element_type=F32) + d_ref[...] * u
    y_ref[...] = jax.nn.gelu(y).astype(y_ref.dtype)


def _s5_sample(u, h0_re, h0_im, bd, cd, lam_row, d_row):
    b, d_ssm = u.shape
    nblk = bd.shape[0]
    cols = SCAN_BLOCK_COLS
    ch = SCAN_BLOCK_CH
    state = pl.BlockSpec((b, cols), lambda j: (0, j))
    return pl.pallas_call(
        _s5_sample_body,
        name="s5_sample",
        grid=(nblk,),
        in_specs=[
            pl.BlockSpec((b, ch), lambda j: (0, j)),
            pl.BlockSpec((1,) + bd.shape[1:], lambda j: (j, 0, 0)),
            pl.BlockSpec((1,) + cd.shape[1:], lambda j: (j, 0, 0)),
            pl.BlockSpec((2, 1, cols), lambda j: (0, 0, j)),
            pl.BlockSpec((1, ch), lambda j: (0, j)),
            state, state,
        ],
        out_specs=[pl.BlockSpec((b, ch), lambda j: (0, j)), state, state],
        out_shape=[jax.ShapeDtypeStruct((b, d_ssm), BF16),
                   jax.ShapeDtypeStruct(h0_re.shape, F32), jax.ShapeDtypeStruct(h0_im.shape, F32)],
        compiler_params=_cparams(("parallel",)),
    )(u, bd, cd, lam_row, d_row, h0_re, h0_im)


def _log_keep(z):
    return -(jnp.maximum(z, 0.0) + jnp.log(1.0 + jnp.exp(-jnp.abs(z))))


def _split_bf16(x):
    hi = x.astype(BF16)
    lo = (x - hi.astype(F32)).astype(BF16)
    return jnp.concatenate([hi, lo], axis=-1)


def _sb_prompt_body(bias_ref, q_ref, k_ref, v_ref, o_ref, *, scale, heads_per_step):
    hp = pl.program_id(1)
    qi = pl.program_id(2)
    tq = q_ref.shape[0]
    dh = SB_HEAD_DIM
    r = lax.broadcasted_iota(jnp.int32, (tq, tq), 0)
    c = lax.broadcasted_iota(jnp.int32, (tq, tq), 1)
    below = r > c
    tri = jnp.where(below, 1.0, 0.0).astype(BF16)
    tri2 = jnp.concatenate([tri, tri], axis=0)
    nh = heads_per_step
    qs = [q_ref[:, hh * dh:(hh + 1) * dh] for hh in range(nh)]
    biases = [bias_ref[hp * nh + hh] for hh in range(nh)]

    def scores(kb):
        rows = pl.ds(pl.multiple_of(kb * tq, tq), tq)
        return tuple(lax.dot_general(qs[hh], k_ref[rows, hh * dh:(hh + 1) * dh], (((1,), (1,)), ((), ())),
                                     preferred_element_type=F32) for hh in range(nh))

    def block(kb, zs, carry, diagonal):
        rows = pl.ds(pl.multiple_of(kb * tq, tq), tq)
        z = [zs[hh] * scale + biases[hh] for hh in range(nh)]
        lk = [_log_keep(z[hh]) for hh in range(nh)]
        log_beta = [z[hh] + lk[hh] for hh in range(nh)]
        if diagonal:
            lk = [jnp.where(below, lk[hh], 0.0) for hh in range(nh)]
        after = [jnp.dot(_split_bf16(lk[hh]), tri2, preferred_element_type=F32) for hh in range(nh)]
        w = [jnp.exp(log_beta[hh] + after[hh] + carry[hh][1]) for hh in range(nh)]
        if diagonal:
            w = [jnp.where(below, w[hh], 0.0) for hh in range(nh)]
        new = []
        for hh in range(nh):
            acc = carry[hh][0] + jnp.dot(w[hh].astype(BF16), v_ref[rows, hh * dh:(hh + 1) * dh],
                                         preferred_element_type=F32)
            run = carry[hh][1] + jnp.sum(lk[hh], axis=-1, keepdims=True)
            new.append((acc, run))
        return tuple(new)

    zero = tuple((jnp.zeros((tq, dh), F32), jnp.zeros((tq, 1), F32)) for _ in range(nh))
    z_next = scores(jnp.maximum(qi - 1, 0))
    carry = block(qi, scores(qi), zero, True)

    def body(i, st):
        carry, zs = st
        kb = qi - 1 - i
        z_next = scores(jnp.maximum(kb - 1, 0))
        return block(kb, zs, carry, False), z_next

    carry, _ = lax.fori_loop(0, qi, body, (carry, z_next))
    for hh in range(nh):
        o_ref[:, hh * dh:(hh + 1) * dh] = carry[hh][0].astype(o_ref.dtype)


def _sb_prompt(q, k, v, bias, n_seq, *, tq):
    m, hd = q.shape
    t = m // n_seq
    hps = SB_PROMPT_HEADS_PER_STEP
    heads = hd // SB_HEAD_DIM
    assert t % tq == 0 and heads % hps == 0
    nq = t // tq
    kv_spec = pl.BlockSpec((t, hps * SB_HEAD_DIM), lambda n, h, i: (n, h))
    q_spec = pl.BlockSpec((tq, hps * SB_HEAD_DIM), lambda n, h, i: (n * nq + i, h))
    return pl.pallas_call(
        functools.partial(_sb_prompt_body, scale=SB_HEAD_DIM ** -0.5, heads_per_step=hps),
        name="sb_prompt",
        grid=(n_seq, heads // hps, nq),
        in_specs=[pl.BlockSpec(memory_space=pltpu.SMEM), q_spec, kv_spec, kv_spec],
        out_specs=q_spec,
        out_shape=jax.ShapeDtypeStruct((m, hd), BF16),
        compiler_params=_cparams(("parallel", "parallel", "arbitrary")),
    )(bias.astype(F32), q, k, v)


def _sb_sample_body(pt_ref, bias_ref, q_ref, knew_ref, vnew_ref, *refs, scale, n_pp, new_key_visible):
    k_refs, v_refs = refs[:n_pp], refs[n_pp:2 * n_pp]
    o_ref, acc, run = refs[2 * n_pp:]
    p = pl.program_id(1)
    heads, dh = q_ref.shape
    page = k_refs[0].shape[0]
    ncol = page * heads
    tile = MXU_DIM
    bias = bias_ref[...]
    q = q_ref[...]

    row = lax.broadcasted_iota(jnp.int32, (heads, ncol), 0)
    col = lax.broadcasted_iota(jnp.int32, (heads, ncol), 1)
    own = lax.rem(col, jnp.full_like(col, heads)) == row
    tr = lax.broadcasted_iota(jnp.int32, (tile, tile), 0)
    tc = lax.broadcasted_iota(jnp.int32, (tile, tile), 1)
    later = lax.div(tr, jnp.full_like(tr, heads)) > lax.div(tc, jnp.full_like(tc, heads))
    tri = jnp.where(later, 1.0, 0.0).astype(BF16)
    tri2 = jnp.concatenate([tri, tri], axis=0)

    @pl.when(p == 0)
    def _():
        z_new = jnp.sum(q.astype(F32) * knew_ref[...], axis=-1, keepdims=True) * scale + bias
        lk_new = _log_keep(z_new)
        w_new = jnp.exp(z_new + lk_new)
        if not new_key_visible:
            lk_new = jnp.zeros_like(lk_new)
            w_new = jnp.zeros_like(w_new)
        run[...] = lk_new
        acc[...] = w_new * vnew_ref[...]

    nt = ncol // tile
    v2 = [v_refs[i][...].reshape(ncol, dh).astype(BF16) for i in range(n_pp)]
    z = [lax.dot_general(q, k_refs[i][...].reshape(ncol, dh).astype(BF16), (((1,), (1,)), ((), ())),
                         preferred_element_type=F32) * scale + bias for i in range(n_pp)]
    lk_raw = [_log_keep(z[i]) for i in range(n_pp)]
    log_beta = [z[i] + lk_raw[i] for i in range(n_pp)]
    lk = [jnp.where(own, lk_raw[i], 0.0) for i in range(n_pp)]
    after = [[jnp.dot(_split_bf16(lk[i][:, t * tile:(t + 1) * tile]), tri2, preferred_element_type=F32)
              for t in range(nt)] for i in range(n_pp)]
    total = [[jnp.sum(lk[i][:, t * tile:(t + 1) * tile], axis=-1, keepdims=True) for t in range(nt)]
             for i in range(n_pp)]
    carry = run[...]
    out = acc[...]
    for i in range(n_pp):
        ws = [None] * nt
        for t in reversed(range(nt)):
            cols = slice(t * tile, (t + 1) * tile)
            ws[t] = jnp.exp(log_beta[i][:, cols] + after[i][t] + carry)
            carry = carry + total[i][t]
        w = jnp.where(own, jnp.concatenate(ws, axis=-1), 0.0)
        out = out + jnp.dot(w.astype(BF16), v2[i], preferred_element_type=F32)
    run[...] = carry
    acc[...] = out

    @pl.when(p == pl.num_programs(1) - 1)
    def _():
        o_ref[...] = out.astype(o_ref.dtype)


def _sb_sample(q, k_new, v_new, bias, pool_k, pool_v, page_table, layer):
    b, heads, dh = q.shape
    page = pool_k.shape[2]
    n_pages = page_table.shape[1]
    n_pp = SB_SAMPLE_PAGES_PER_STEP
    assert n_pages % n_pp == 0 and (page * heads) % MXU_DIM == 0 and MXU_DIM % heads == 0
    past = n_pages * page
    new_key_pos, query_pos = past, past
    new_key_visible = new_key_pos < query_pos
    row = pl.BlockSpec((None, heads, dh), lambda n, p, pt: (n, 0, 0))

    def page_spec(i):
        return pl.BlockSpec(
            (None, None, page, heads, dh),
            lambda n, p, pt: (layer, pt[n * n_pages + n_pages - 1 - (p * n_pp + i)], 0, 0, 0))

    grid_spec = pltpu.PrefetchScalarGridSpec(
        num_scalar_prefetch=1,
        grid=(b, n_pages // n_pp),
        in_specs=[pl.BlockSpec((heads, 1), lambda n, p, pt: (0, 0)), row, row, row]
                 + [page_spec(i) for i in range(n_pp)] * 2,
        out_specs=row,
        scratch_shapes=[pltpu.VMEM((heads, dh), F32), pltpu.VMEM((heads, 1), F32)],
    )
    return pl.pallas_call(
        functools.partial(_sb_sample_body, scale=SB_HEAD_DIM ** -0.5, n_pp=n_pp, new_key_visible=new_key_visible),
        name="sb_sample",
        grid_spec=grid_spec,
        out_shape=jax.ShapeDtypeStruct((b, heads, dh), BF16),
        compiler_params=_cparams(("parallel", "arbitrary")),
    )(page_table.reshape(-1).astype(jnp.int32), bias.astype(F32).reshape(heads, 1), q, k_new, v_new,
      *([pool_k] * n_pp), *([pool_v] * n_pp))


def _xattn_prompt_body(q_ref, k_ref, v_ref, o_ref, *, scale):
    q = q_ref[...]
    kb = k_ref[...].astype(BF16)
    vb = v_ref[...].astype(BF16)
    s = lax.dot_general(q, kb, (((1,), (1,)), ((), ())), preferred_element_type=F32) * scale
    s = s - jnp.max(s, axis=-1, keepdims=True)
    e = jnp.exp(s)
    p = e / jnp.sum(e, axis=-1, keepdims=True)
    o_ref[...] = jnp.dot(p.astype(BF16), vb, preferred_element_type=F32).astype(o_ref.dtype)


def _xattn_prompt(q, mem_k, mem_v, n_seq, *, tq):
    m, d = q.shape
    t = m // n_seq
    hd = d // MEM_HEADS
    n_mem = mem_k.shape[0] // n_seq
    nq = t // tq
    q_spec = pl.BlockSpec((tq, hd), lambda n, h, i: (n * nq + i, h))
    kv_spec = pl.BlockSpec((n_mem, hd), lambda n, h, i: (n, h))
    return pl.pallas_call(
        functools.partial(_xattn_prompt_body, scale=hd ** -0.5),
        name="xattn_prompt",
        grid=(n_seq, MEM_HEADS, nq),
        in_specs=[q_spec, kv_spec, kv_spec],
        out_specs=q_spec,
        out_shape=jax.ShapeDtypeStruct((m, d), BF16),
        compiler_params=_cparams(("parallel", "parallel", "parallel")),
    )(q, mem_k, mem_v)


def _xattn_sample_body(q_ref, k_ref, v_ref, o_ref, *, scale):
    q = q_ref[...]
    s = jnp.sum(k_ref[...] * q[None], axis=-1, keepdims=True) * scale
    s = s - jnp.max(s, axis=0, keepdims=True)
    e = jnp.exp(s)
    p = e / jnp.sum(e, axis=0, keepdims=True)
    o_ref[...] = jnp.sum(p * v_ref[...], axis=0)


def _xattn_sample(q, mem_k, mem_v, layer):
    b, heads, hd = q.shape
    n_mem = mem_k.shape[2]
    row = pl.BlockSpec((None, heads, hd), lambda n: (n, 0, 0))
    kv = pl.BlockSpec((None, None, n_mem, heads, hd), lambda n: (layer, n, 0, 0, 0))
    return pl.pallas_call(
        functools.partial(_xattn_sample_body, scale=hd ** -0.5),
        name="xattn_sample",
        grid=(b,),
        in_specs=[row, kv, kv],
        out_specs=row,
        out_shape=jax.ShapeDtypeStruct((b, heads, hd), F32),
        compiler_params=_cparams(("parallel",)),
    )(q, mem_k, mem_v)


def _layer(x, w, *, tag, tm, tn, ff2_tiles, sb_fn, s5_fn, xattn_fn, alpha):
    m, d = x.shape
    d_ssm = w['w_glu_val'].shape[0]
    d_sb = w['w_sb_branch'].shape[0]

    def mm(a, bs, name, **kw):
        return _matmul(a, bs, name=f"{tag}_{name}", tm=tm, tn=tn, **kw)

    xb = x.astype(BF16)
    w_in = w['w_in']
    u, = mm(xb, [w_in], "in_u", n=d_ssm, b_col=0, epilogue=_epi_plain, out_dtypes=[F32])
    q, = mm(xb, [w_in], "in_q", n=d_sb, b_col=d_ssm, epilogue=_epi_plain, out_dtypes=[BF16])
    k, kb = mm(xb, [w_in], "in_k", n=d_sb, b_col=d_ssm + d_sb, epilogue=_epi_two_copies, out_dtypes=[F32, BF16])
    v, vb = mm(xb, [w_in], "in_v", n=d_sb, b_col=d_ssm + 2 * d_sb, epilogue=_epi_two_copies,
               out_dtypes=[F32, BF16])
    gates, = mm(xb, [w_in], "in_gates", n=2 * d, b_col=d_ssm + 3 * d_sb, epilogue=_epi_sigmoid, out_dtypes=[F32])

    y, s5_state = s5_fn(u)
    ssm, = mm(y, [w['w_glu_val'], w['w_glu_gate']], "glu", n=d, epilogue=_epi_glu,
              extras=[(gates, 'tile', 0)], out_dtypes=[F32])
    sb = sb_fn(q, k, v, kb, vb)
    merged, = mm(sb, [w['w_sb_branch']], "merge", n=d, epilogue=_epi_merge,
                 extras=[(ssm, 'tile', 0), (gates, 'tile', d)], out_dtypes=[BF16])
    residual = functools.partial(_epi_residual, alpha=alpha)
    s1, = mm(merged, [w['w_mix_out']], "mix_out", n=d, epilogue=residual, extras=[(x, 'tile', 0)],
             out_dtypes=[F32])
    ln_tm = min(tm, 256)
    x1, x1b = _layer_norm(s1, w['ln1_g'], w['ln1_b'], name=f"{tag}_ln1", tm=ln_tm, out_dtypes=[F32, BF16])

    qm, = mm(x1b, [w['w_mem_q']], "mem_q", n=d, epilogue=_epi_plain, out_dtypes=[BF16])
    o = xattn_fn(qm)
    s2, = mm(o, [w['w_mem_o']], "mem_o", n=d, epilogue=residual, extras=[(x1, 'tile', 0)], out_dtypes=[F32])
    x2, x2b = _layer_norm(s2, w['ln2_g'], w['ln2_b'], name=f"{tag}_ln2", tm=ln_tm, out_dtypes=[F32, BF16])

    hmid, = mm(x2b, [w['w_ff1']], "ff1", n=w['w_ff1'].shape[1], epilogue=_epi_relu2, out_dtypes=[BF16])
    s3, = _matmul(hmid, [w['w_ff2_bf16']], name=f"{tag}_ff2", n=d, **ff2_tiles, epilogue=residual,
                  extras=[(x2, 'tile', 0)], out_dtypes=[F32])
    x3, = _layer_norm(s3, w['ln3_g'], w['ln3_b'], name=f"{tag}_ln3", tm=ln_tm, out_dtypes=[F32])
    return x3, k, v, s5_state


def kernel(x_prompt, x_sample, cache_k, cache_v, cache_mem_k, cache_mem_v, state_ssm_re, state_ssm_im, page_table, mem_prompt, w_in, ssm_a_re, ssm_a_im, ssm_log_dt, ssm_b_re, ssm_b_im, ssm_c_re, ssm_c_im, ssm_d, w_glu_val, w_glu_gate, sb_bias, w_sb_branch, w_mix_out, ln1_g, ln1_b, w_mem_q, w_mem_k, w_mem_v, w_mem_o, ln2_g, ln2_b, w_ff1, w_ff2, ln3_g, ln3_b):
    depth = w_in.shape[0]
    n_seq, seq, d = x_prompt.shape
    b_dec, dec_seq, _ = x_sample.shape
    assert dec_seq == 1, "the decode group is one token per sequence"
    n_mem = mem_prompt.shape[1]
    alpha = (2 * depth) ** 0.25
    heads = sb_bias.shape[1]
    groups = ssm_a_re.shape[1]

    xp = x_prompt.reshape(n_seq * seq, d)
    xs = x_sample.reshape(b_dec, d)
    memb = mem_prompt.reshape(n_seq * n_mem, d).astype(BF16)
    outs = {name: [] for name in ('kp', 'vp', 'hrp', 'hip', 'mkp', 'mvp', 'ks', 'vs', 'hrs', 'his')}
    for l in range(depth):
        w = dict(w_in=w_in[l], w_glu_val=w_glu_val[l], w_glu_gate=w_glu_gate[l], w_sb_branch=w_sb_branch[l],
                 w_mix_out=w_mix_out[l], ln1_g=ln1_g[l], ln1_b=ln1_b[l], w_mem_q=w_mem_q[l], w_mem_o=w_mem_o[l],
                 ln2_g=ln2_g[l], ln2_b=ln2_b[l], w_ff1=w_ff1[l], w_ff2_bf16=w_ff2[l].astype(BF16),
                 ln3_g=ln3_g[l], ln3_b=ln3_b[l])
        lb_re, lb_im, bb_re, bb_im = _s5_discretise(ssm_a_re[l], ssm_a_im[l], ssm_log_dt[l], ssm_b_re[l], ssm_b_im[l])
        bd, cd, lam_scan, lam_row, d_row = _s5_layout(lb_re, lb_im, bb_re, bb_im, ssm_c_re[l], ssm_c_im[l], ssm_d[l])

        mem_tiles = dict(n=d, tm=n_seq * n_mem, tn=256, epilogue=_epi_plain, out_dtypes=[F32])
        mkp, = _matmul(memb, [w_mem_k[l]], name="prompt_mem_k", **mem_tiles)
        mvp, = _matmul(memb, [w_mem_v[l]], name="prompt_mem_v", **mem_tiles)

        def s5_prompt(u):
            y, hre, him = _s5_prompt(u, n_seq, bd, cd, lam_scan, d_row)
            return y, (hre, him)

        xp, kp, vp, (hrp, hip) = _layer(
            xp, w, tag="prompt", tm=1024, tn=512, ff2_tiles=dict(tm=1024, tn=512, tk=2048), alpha=alpha,
            s5_fn=s5_prompt,
            sb_fn=lambda q, k, v, kb, vb: _sb_prompt(q, kb, vb, sb_bias[l], n_seq, tq=256),
            xattn_fn=lambda qm: _xattn_prompt(qm, mkp, mvp, n_seq, tq=512))

        def s5_sample(u):
            y, hre, him = _s5_sample(u, state_ssm_re[l].reshape(b_dec, -1), state_ssm_im[l].reshape(b_dec, -1),
                                     bd, cd, lam_row, d_row)
            return y, (hre.reshape(b_dec, groups, -1), him.reshape(b_dec, groups, -1))

        def sb_sample(q, k, v, kb, vb):
            per_head = lambda a: a.reshape(b_dec, heads, -1)
            return _sb_sample(per_head(q), per_head(k), per_head(v), sb_bias[l], cache_k, cache_v,
                              page_table, l).reshape(b_dec, -1)

        def xattn_sample(qm):
            o = _xattn_sample(qm.astype(F32).reshape(b_dec, MEM_HEADS, -1), cache_mem_k, cache_mem_v, l)
            return o.reshape(b_dec, d).astype(BF16)

        xs, ks, vs, (hrs, his) = _layer(
            xs, w, tag="sample", tm=b_dec, tn=512, ff2_tiles=dict(tm=b_dec, tn=512), alpha=alpha,
            s5_fn=s5_sample, sb_fn=sb_sample, xattn_fn=xattn_sample)

        outs['kp'].append(kp.reshape(n_seq, seq, heads, -1))
        outs['vp'].append(vp.reshape(n_seq, seq, heads, -1))
        outs['hrp'].append(hrp)
        outs['hip'].append(hip)
        outs['mkp'].append(mkp.reshape(n_seq, n_mem, MEM_HEADS, -1))
        outs['mvp'].append(mvp.reshape(n_seq, n_mem, MEM_HEADS, -1))
        outs['ks'].append(ks.reshape(b_dec, dec_seq, heads, -1))
        outs['vs'].append(vs.reshape(b_dec, dec_seq, heads, -1))
        outs['hrs'].append(hrs)
        outs['his'].append(his)

    st = {name: jnp.stack(vals) for name, vals in outs.items()}
    return (xp.reshape(n_seq, seq, d), xs.reshape(b_dec, dec_seq, d),
            st['kp'], st['vp'], st['hrp'], st['hip'], st['mkp'], st['mvp'],
            st['ks'], st['vs'], st['hrs'], st['his'])
```

```python
import functools

import jax
import jax.numpy as jnp
from jax import lax
from jax.experimental import pallas as pl
from jax.experimental.pallas import tpu as pltpu

F32 = jnp.float32
BF16 = jnp.bfloat16

LN_EPS = 1e-5
SSM_GROUP = 16
SSM_STATE = 64
SB_HEAD_DIM = 128
MEM_HEADS = 4

LANES = 128
SUBLANES = 8
MXU_DIM = 256
VMEM_LIMIT_BYTES = 60 * 1024 * 1024

SCAN_SLABS = 8
SCAN_BLOCK_COLS = SCAN_SLABS * LANES
SCAN_BLOCK_CH = SCAN_BLOCK_COLS // SSM_STATE * SSM_GROUP
SCAN_CHUNK = 256
SCAN_PITCH = SCAN_CHUNK + SUBLANES

SB_PROMPT_HEADS_PER_STEP = 8
SB_SAMPLE_PAGES_PER_STEP = 4


def _cparams(semantics):
    return pltpu.CompilerParams(dimension_semantics=semantics, vmem_limit_bytes=VMEM_LIMIT_BYTES)


def _mm_body(*refs, n_b, n_extra, n_out, nk, epilogue):
    a_ref = refs[0]
    b_refs = refs[1:1 + n_b]
    e_refs = refs[1 + n_b:1 + n_b + n_extra]
    o_refs = refs[1 + n_b + n_extra:1 + n_b + n_extra + n_out]
    acc_refs = refs[1 + n_b + n_extra + n_out:]
    a = a_ref[...]
    parts = [jnp.dot(a, b[...].astype(BF16), preferred_element_type=F32) for b in b_refs]

    def finish(accs):
        outs = epilogue(accs, [e[...] for e in e_refs])
        for o_ref, o in zip(o_refs, outs):
            o_ref[...] = o.astype(o_ref.dtype)

    if nk == 1:
        finish(parts)
    else:
        k = pl.program_id(2)

        @pl.when(k == 0)
        def _():
            for acc_ref, p in zip(acc_refs, parts):
                acc_ref[...] = p

        @pl.when(k > 0)
        def _():
            for acc_ref, p in zip(acc_refs, parts):
                acc_ref[...] += p

        @pl.when(k == nk - 1)
        def _():
            finish([acc_ref[...] for acc_ref in acc_refs])


def _matmul(a, bs, *, name, n, tm, tn, tk=None, b_col=0, epilogue, extras=(), out_dtypes):
    m, k_dim = a.shape
    tk = k_dim if tk is None else tk
    assert m % tm == 0 and n % tn == 0 and k_dim % tk == 0 and b_col % tn == 0
    nk = k_dim // tk
    boff = b_col // tn
    in_specs = [pl.BlockSpec((tm, tk), lambda i, j, k: (i, k))]
    for _ in bs:
        in_specs.append(pl.BlockSpec((tk, tn), lambda i, j, k: (k, j + boff)))
    operands = [a, *bs]
    for arr, kind, col in extras:
        assert col % tn == 0
        eoff = col // tn
        if kind == 'tile':
            in_specs.append(pl.BlockSpec((tm, tn), lambda i, j, k, eoff=eoff: (i, j + eoff)))
        else:
            in_specs.append(pl.BlockSpec((1, tn), lambda i, j, k, eoff=eoff: (0, j + eoff)))
        operands.append(arr)
    out_shape = [jax.ShapeDtypeStruct((m, n), dt) for dt in out_dtypes]
    out_specs = [pl.BlockSpec((tm, tn), lambda i, j, k: (i, j)) for _ in out_dtypes]
    scratch = [pltpu.VMEM((tm, tn), F32) for _ in bs] if nk > 1 else []
    body = functools.partial(_mm_body, n_b=len(bs), n_extra=len(extras), n_out=len(out_dtypes),
                             nk=nk, epilogue=epilogue)
    outs = pl.pallas_call(
        body,
        name=name,
        grid=(m // tm, n // tn, nk),
        in_specs=in_specs,
        out_specs=out_specs,
        out_shape=out_shape,
        scratch_shapes=scratch,
        compiler_params=_cparams(("parallel", "parallel", "arbitrary")),
    )(*operands)
    return outs


def _epi_plain(accs, extras):
    return [accs[0]]


def _epi_two_copies(accs, extras):
    return [accs[0], accs[0]]


def _epi_sigmoid(accs, extras):
    return [jax.nn.sigmoid(accs[0])]


def _epi_glu(accs, extras):
    return [extras[0] * (accs[0] * jax.nn.sigmoid(accs[1]))]


def _epi_merge(accs, extras):
    return [extras[0] + extras[1] * accs[0]]


def _epi_residual(accs, extras, *, alpha):
    return [alpha * extras[0] + accs[0]]


def _epi_relu2(accs, extras):
    return [jnp.square(jnp.maximum(accs[0], 0.0))]


def _ln_body(s_ref, g_ref, b_ref, *o_refs):
    x = s_ref[...]
    mu = jnp.mean(x, axis=-1, keepdims=True)
    xc = x - mu
    var = jnp.mean(xc * xc, axis=-1, keepdims=True)
    y = xc * lax.rsqrt(var + LN_EPS) * g_ref[...] + b_ref[...]
    for o_ref in o_refs:
        o_ref[...] = y.astype(o_ref.dtype)


def _layer_norm(s, g, b, *, name, tm, out_dtypes):
    m, d = s.shape
    assert m % tm == 0
    row = pl.BlockSpec((tm, d), lambda i: (i, 0))
    vec = pl.BlockSpec((1, d), lambda i: (0, 0))
    return pl.pallas_call(
        _ln_body,
        name=name,
        grid=(m // tm,),
        in_specs=[row, vec, vec],
        out_specs=[row for _ in out_dtypes],
        out_shape=[jax.ShapeDtypeStruct((m, d), dt) for dt in out_dtypes],
        compiler_params=_cparams(("parallel",)),
    )(s, g.reshape(1, d).astype(F32), b.reshape(1, d).astype(F32))


def _s5_discretise(a_re, a_im, log_dt, b_re, b_im):
    a_re, a_im = a_re.astype(F32), a_im.astype(F32)
    b_re, b_im = b_re.astype(F32), b_im.astype(F32)
    dt = jnp.exp(log_dt.astype(F32))[:, None]
    ld_re, ld_im = a_re * dt, a_im * dt
    mag = jnp.exp(ld_re)
    lb_re, lb_im = mag * jnp.cos(ld_im), mag * jnp.sin(ld_im)
    den = a_re * a_re + a_im * a_im
    f_re = ((lb_re - 1.0) * a_re + lb_im * a_im) / den
    f_im = (lb_im * a_re - (lb_re - 1.0) * a_im) / den
    bb_re = f_re[..., None] * b_re - f_im[..., None] * b_im
    bb_im = f_re[..., None] * b_im + f_im[..., None] * b_re
    return lb_re, lb_im, bb_re, bb_im


def _s5_layout(lb_re, lb_im, bb_re, bb_im, c_re, c_im, d_skip):
    g, p, c = bb_re.shape
    nblk = g * p // SCAN_BLOCK_COLS
    gpb = g // nblk
    eye = jnp.eye(gpb, dtype=F32)

    def expand_b(bb):
        bb4 = bb.reshape(nblk, gpb, p, c)
        return jnp.einsum('jgpc,gh->jgchp', bb4, eye).reshape(nblk, gpb * c, gpb * p)

    def expand_c(cc):
        c4 = cc.astype(F32).reshape(nblk, gpb, c, p)
        return jnp.einsum('jgcp,gh->jgphc', c4, eye).reshape(nblk, gpb * p, gpb * c)

    bd = jnp.concatenate([expand_b(bb_re), expand_b(bb_im)], axis=2).astype(BF16)
    cd = jnp.concatenate([expand_c(c_re), -expand_c(c_im)], axis=1).astype(BF16)

    def scan_layout(lb):
        return lb.reshape(nblk, SCAN_SLABS, LANES).transpose(1, 0, 2)

    lam_scan = jnp.stack([scan_layout(lb_re), scan_layout(lb_im)])
    lam_row = jnp.stack([lb_re.reshape(1, g * p), lb_im.reshape(1, g * p)])
    d_row = d_skip.astype(F32).reshape(1, g * c)
    return bd, cd, lam_scan, lam_row, d_row


def _s5_prompt_body(u_ref, bd_ref, cd_ref, lam_ref, d_ref, y_ref, hre_ref, him_ref, s_re, s_im, hc):
    ti = pl.program_id(1)
    nblk = bd_ref.shape[0]
    tc = u_ref.shape[0]
    half = SCAN_BLOCK_COLS

    @pl.when(ti == 0)
    def _():
        hc[...] = jnp.zeros_like(hc)

    ub = u_ref[...].astype(BF16)
    for j in range(nblk):
        bu = jnp.dot(ub[:, j * SCAN_BLOCK_CH:(j + 1) * SCAN_BLOCK_CH], bd_ref[j], preferred_element_type=F32)
        for k in range(SCAN_SLABS):
            s_re[k, j * SCAN_PITCH:j * SCAN_PITCH + tc, :] = bu[:, k * LANES:(k + 1) * LANES]
            s_im[k, j * SCAN_PITCH:j * SCAN_PITCH + tc, :] = bu[:, half + k * LANES:half + (k + 1) * LANES]

    lr = [lam_ref[0, k] for k in range(SCAN_SLABS)]
    li = [lam_ref[1, k] for k in range(SCAN_SLABS)]

    def step(t, carry):
        hr, hi = carry
        new_r, new_i = [], []
        for k in range(SCAN_SLABS):
            rows = pl.ds(t, SUBLANES, stride=SCAN_PITCH)
            nr = lr[k] * hr[k] - li[k] * hi[k] + s_re[k, rows, :]
            ni = lr[k] * hi[k] + li[k] * hr[k] + s_im[k, rows, :]
            s_re[k, rows, :] = nr
            s_im[k, rows, :] = ni
            new_r.append(nr)
            new_i.append(ni)
        return tuple(new_r), tuple(new_i)

    init = (tuple(hc[0, k] for k in range(SCAN_SLABS)), tuple(hc[1, k] for k in range(SCAN_SLABS)))
    hr, hi = lax.fori_loop(0, tc, step, init)
    for k in range(SCAN_SLABS):
        hc[0, k] = hr[k]
        hc[1, k] = hi[k]

    for j in range(nblk):
        rows = slice(j * SCAN_PITCH, j * SCAN_PITCH + tc)
        hcat = jnp.concatenate([s_re[k, rows, :] for k in range(SCAN_SLABS)]
                               + [s_im[k, rows, :] for k in range(SCAN_SLABS)], axis=-1).astype(BF16)
        cols = slice(j * SCAN_BLOCK_CH, (j + 1) * SCAN_BLOCK_CH)
        y = jnp.dot(hcat, cd_ref[j], preferred_element_type=F32) + d_ref[:, cols] * u_ref[:, cols]
        y_ref[:, cols] = jax.nn.gelu(y).astype(y_ref.dtype)

    @pl.when(ti == pl.num_programs(1) - 1)
    def _():
        hre_ref[0] = hc[0]
        him_ref[0] = hc[1]


def _s5_prompt(u, n_seq, bd, cd, lam_scan, d_row):
    m, d_ssm = u.shape
    t = m // n_seq
    tc = SCAN_CHUNK
    assert t % tc == 0
    nblk = bd.shape[0]
    nt = t // tc
    const3 = lambda shape: pl.BlockSpec(shape, lambda n, i: (0, 0, 0), pipeline_mode=pl.Buffered(1))
    state_shape = jax.ShapeDtypeStruct((n_seq, SCAN_SLABS, SUBLANES, LANES), F32)
    state_spec = pl.BlockSpec((1, SCAN_SLABS, SUBLANES, LANES), lambda n, i: (n, 0, 0, 0))
    y, hre, him = pl.pallas_call(
        _s5_prompt_body,
        name="s5_prompt",
        grid=(n_seq, nt),
        in_specs=[
            pl.BlockSpec((tc, d_ssm), lambda n, i: (n * nt + i, 0)),
            const3(bd.shape),
            const3(cd.shape),
            pl.BlockSpec(lam_scan.shape, lambda n, i: (0, 0, 0, 0), pipeline_mode=pl.Buffered(1)),
            pl.BlockSpec(d_row.shape, lambda n, i: (0, 0)),
        ],
        out_specs=[pl.BlockSpec((tc, d_ssm), lambda n, i: (n * nt + i, 0)), state_spec, state_spec],
        out_shape=[jax.ShapeDtypeStruct((m, d_ssm), BF16), state_shape, state_shape],
        scratch_shapes=[
            pltpu.VMEM((SCAN_SLABS, nblk * SCAN_PITCH, LANES), F32),
            pltpu.VMEM((SCAN_SLABS, nblk * SCAN_PITCH, LANES), F32),
            pltpu.VMEM((2, SCAN_SLABS, SUBLANES, LANES), F32),
        ],
        compiler_params=_cparams(("arbitrary", "arbitrary")),
    )(u, bd, cd, lam_scan, d_row)

    def to_gp(h):
        return h.transpose(0, 2, 1, 3).reshape(n_seq, -1, SSM_STATE)

    return y, to_gp(hre), to_gp(him)


def _s5_sample_body(u_ref, bd_ref, cd_ref, lam_ref, d_ref, h0re_ref, h0im_ref, y_ref, hre_ref, him_ref):
    half = SCAN_BLOCK_COLS
    u = u_ref[...]
    bu = jnp.dot(u.astype(BF16), bd_ref[0], preferred_element_type=F32)
    lr, li = lam_ref[0], lam_ref[1]
    h0r, h0i = h0re_ref[...], h0im_ref[...]
    hr = bu[:, :half] + (lr * h0r - li * h0i)
    hi = bu[:, half:] + (lr * h0i + li * h0r)
    hre_ref[...] = hr
    him_ref[...] = hi
    hcat = jnp.concatenate([hr, hi], axis=-1).astype(BF16)
    y = jnp.dot(hcat, cd_ref[0], preferred_element_type=F32) + d_ref[...] * u
    y_ref[...] = jax.nn.gelu(y).astype(y_ref.dtype)


def _s5_sample(u, h0_re, h0_im, bd, cd, lam_row, d_row):
    b, d_ssm = u.shape
    nblk = bd.shape[0]
    cols = SCAN_BLOCK_COLS
    ch = SCAN_BLOCK_CH
    state = pl.BlockSpec((b, cols), lambda j: (0, j))
    return pl.pallas_call(
        _s5_sample_body,
        name="s5_sample",
        grid=(nblk,),
        in_specs=[
            pl.BlockSpec((b, ch), lambda j: (0, j)),
            pl.BlockSpec((1,) + bd.shape[1:], lambda j: (j, 0, 0)),
            pl.BlockSpec((1,) + cd.shape[1:], lambda j: (j, 0, 0)),
            pl.BlockSpec((2, 1, cols), lambda j: (0, 0, j)),
            pl.BlockSpec((1, ch), lambda j: (0, j)),
            state, state,
        ],
        out_specs=[pl.BlockSpec((b, ch), lambda j: (0, j)), state, state],
        out_shape=[jax.ShapeDtypeStruct((b, d_ssm), BF16),
                   jax.ShapeDtypeStruct(h0_re.shape, F32), jax.ShapeDtypeStruct(h0_im.shape, F32)],
        compiler_params=_cparams(("parallel",)),
    )(u, bd, cd, lam_row, d_row, h0_re, h0_im)


def _log_keep(z):
    return -(jnp.maximum(z, 0.0) + jnp.log(1.0 + jnp.exp(-jnp.abs(z))))


def _split_bf16(x):
    hi = x.astype(BF16)
    lo = (x - hi.astype(F32)).astype(BF16)
    return jnp.concatenate([hi, lo], axis=-1)


def _sb_prompt_body(bias_ref, q_ref, k_ref, v_ref, o_ref, *, scale, heads_per_step):
    hp = pl.program_id(1)
    qi = pl.program_id(2)
    tq = q_ref.shape[0]
    dh = SB_HEAD_DIM
    r = lax.broadcasted_iota(jnp.int32, (tq, tq), 0)
    c = lax.broadcasted_iota(jnp.int32, (tq, tq), 1)
    below = r > c
    tri = jnp.where(below, 1.0, 0.0).astype(BF16)
    tri2 = jnp.concatenate([tri, tri], axis=0)
    nh = heads_per_step
    qs = [q_ref[:, hh * dh:(hh + 1) * dh] for hh in range(nh)]
    biases = [bias_ref[hp * nh + hh] for hh in range(nh)]

    def scores(kb):
        rows = pl.ds(pl.multiple_of(kb * tq, tq), tq)
        return tuple(lax.dot_general(qs[hh], k_ref[rows, hh * dh:(hh + 1) * dh], (((1,), (1,)), ((), ())),
                                     preferred_element_type=F32) for hh in range(nh))

    def block(kb, zs, carry, diagonal):
        rows = pl.ds(pl.multiple_of(kb * tq, tq), tq)
        z = [zs[hh] * scale + biases[hh] for hh in range(nh)]
        lk = [_log_keep(z[hh]) for hh in range(nh)]
        log_beta = [z[hh] + lk[hh] for hh in range(nh)]
        if diagonal:
            lk = [jnp.where(below, lk[hh], 0.0) for hh in range(nh)]
        after = [jnp.dot(_split_bf16(lk[hh]), tri2, preferred_element_type=F32) for hh in range(nh)]
        w = [jnp.exp(log_beta[hh] + after[hh] + carry[hh][1]) for hh in range(nh)]
        if diagonal:
            w = [jnp.where(below, w[hh], 0.0) for hh in range(nh)]
        new = []
        for hh in range(nh):
            acc = carry[hh][0] + jnp.dot(w[hh].astype(BF16), v_ref[rows, hh * dh:(hh + 1) * dh],
                                         preferred_element_type=F32)
            run = carry[hh][1] + jnp.sum(lk[hh], axis=-1, keepdims=True)
            new.append((acc, run))
        return tuple(new)

    zero = tuple((jnp.zeros((tq, dh), F32), jnp.zeros((tq, 1), F32)) for _ in range(nh))
    carry = block(qi, scores(qi), zero, True)
    carry = lax.fori_loop(0, qi, lambda i, cr: block(qi - 1 - i, scores(qi - 1 - i), cr, False), carry)
    for hh in range(nh):
        o_ref[:, hh * dh:(hh + 1) * dh] = carry[hh][0].astype(o_ref.dtype)


def _sb_prompt(q, k, v, bias, n_seq, *, tq):
    m, hd = q.shape
    t = m // n_seq
    hps = SB_PROMPT_HEADS_PER_STEP
    heads = hd // SB_HEAD_DIM
    assert t % tq == 0 and heads % hps == 0
    nq = t // tq
    kv_spec = pl.BlockSpec((t, hps * SB_HEAD_DIM), lambda n, h, i: (n, h))
    q_spec = pl.BlockSpec((tq, hps * SB_HEAD_DIM), lambda n, h, i: (n * nq + i, h))
    return pl.pallas_call(
        functools.partial(_sb_prompt_body, scale=SB_HEAD_DIM ** -0.5, heads_per_step=hps),
        name="sb_prompt",
        grid=(n_seq, heads // hps, nq),
        in_specs=[pl.BlockSpec(memory_space=pltpu.SMEM), q_spec, kv_spec, kv_spec],
        out_specs=q_spec,
        out_shape=jax.ShapeDtypeStruct((m, hd), BF16),
        compiler_params=_cparams(("parallel", "parallel", "arbitrary")),
    )(bias.astype(F32), q, k, v)


def _sb_sample_body(pt_ref, bias_ref, q_ref, knew_ref, vnew_ref, *refs, scale, n_pp, new_key_visible):
    k_refs, v_refs = refs[:n_pp], refs[n_pp:2 * n_pp]
    o_ref, acc, run = refs[2 * n_pp:]
    p = pl.program_id(1)
    heads, dh = q_ref.shape
    page = k_refs[0].shape[0]
    ncol = page * heads
    tile = MXU_DIM
    bias = bias_ref[...]
    q = q_ref[...]

    row = lax.broadcasted_iota(jnp.int32, (heads, ncol), 0)
    col = lax.broadcasted_iota(jnp.int32, (heads, ncol), 1)
    own = lax.rem(col, jnp.full_like(col, heads)) == row
    tr = lax.broadcasted_iota(jnp.int32, (tile, tile), 0)
    tc = lax.broadcasted_iota(jnp.int32, (tile, tile), 1)
    later = lax.div(tr, jnp.full_like(tr, heads)) > lax.div(tc, jnp.full_like(tc, heads))
    tri = jnp.where(later, 1.0, 0.0).astype(BF16)
    tri2 = jnp.concatenate([tri, tri], axis=0)

    @pl.when(p == 0)
    def _():
        z_new = jnp.sum(q.astype(F32) * knew_ref[...], axis=-1, keepdims=True) * scale + bias
        lk_new = _log_keep(z_new)
        w_new = jnp.exp(z_new + lk_new)
        if not new_key_visible:
            lk_new = jnp.zeros_like(lk_new)
            w_new = jnp.zeros_like(w_new)
        run[...] = lk_new
        acc[...] = w_new * vnew_ref[...]

    nt = ncol // tile
    v2 = [v_refs[i][...].reshape(ncol, dh).astype(BF16) for i in range(n_pp)]
    z = [lax.dot_general(q, k_refs[i][...].reshape(ncol, dh).astype(BF16), (((1,), (1,)), ((), ())),
                         preferred_element_type=F32) * scale + bias for i in range(n_pp)]
    lk_raw = [_log_keep(z[i]) for i in range(n_pp)]
    log_beta = [z[i] + lk_raw[i] for i in range(n_pp)]
    lk = [jnp.where(own, lk_raw[i], 0.0) for i in range(n_pp)]
    after = [[jnp.dot(_split_bf16(lk[i][:, t * tile:(t + 1) * tile]), tri2, preferred_element_type=F32)
              for t in range(nt)] for i in range(n_pp)]
    total = [[jnp.sum(lk[i][:, t * tile:(t + 1) * tile], axis=-1, keepdims=True) for t in range(nt)]
             for i in range(n_pp)]
    carry = run[...]
    out = acc[...]
    for i in range(n_pp):
        ws = [None] * nt
        for t in reversed(range(nt)):
            cols = slice(t * tile, (t + 1) * tile)
            ws[t] = jnp.exp(log_beta[i][:, cols] + after[i][t] + carry)
            carry = carry + total[i][t]
        w = jnp.where(own, jnp.concatenate(ws, axis=-1), 0.0)
        out = out + jnp.dot(w.astype(BF16), v2[i], preferred_element_type=F32)
    run[...] = carry
    acc[...] = out

    @pl.when(p == pl.num_programs(1) - 1)
    def _():
        o_ref[...] = out.astype(o_ref.dtype)


def _sb_sample(q, k_new, v_new, bias, pool_k, pool_v, page_table, layer):
    b, heads, dh = q.shape
    page = pool_k.shape[2]
    n_pages = page_table.shape[1]
    n_pp = SB_SAMPLE_PAGES_PER_STEP
    assert n_pages % n_pp == 0 and (page * heads) % MXU_DIM == 0 and MXU_DIM % heads == 0
    past = n_pages * page
    new_key_pos, query_pos = past, past
    new_key_visible = new_key_pos < query_pos
    row = pl.BlockSpec((None, heads, dh), lambda n, p, pt: (n, 0, 0))

    def page_spec(i):
        return pl.BlockSpec(
            (None, None, page, heads, dh),
            lambda n, p, pt: (layer, pt[n * n_pages + n_pages - 1 - (p * n_pp + i)], 0, 0, 0))

    grid_spec = pltpu.PrefetchScalarGridSpec(
        num_scalar_prefetch=1,
        grid=(b, n_pages // n_pp),
        in_specs=[pl.BlockSpec((heads, 1), lambda n, p, pt: (0, 0)), row, row, row]
                 + [page_spec(i) for i in range(n_pp)] * 2,
        out_specs=row,
        scratch_shapes=[pltpu.VMEM((heads, dh), F32), pltpu.VMEM((heads, 1), F32)],
    )
    return pl.pallas_call(
        functools.partial(_sb_sample_body, scale=SB_HEAD_DIM ** -0.5, n_pp=n_pp, new_key_visible=new_key_visible),
        name="sb_sample",
        grid_spec=grid_spec,
        out_shape=jax.ShapeDtypeStruct((b, heads, dh), BF16),
        compiler_params=_cparams(("parallel", "arbitrary")),
    )(page_table.reshape(-1).astype(jnp.int32), bias.astype(F32).reshape(heads, 1), q, k_new, v_new,
      *([pool_k] * n_pp), *([pool_v] * n_pp))


def _xattn_prompt_body(q_ref, k_ref, v_ref, o_ref, *, scale):
    q = q_ref[...]
    kb = k_ref[...].astype(BF16)
    vb = v_ref[...].astype(BF16)
    s = lax.dot_general(q, kb, (((1,), (1,)), ((), ())), preferred_element_type=F32) * scale
    s = s - jnp.max(s, axis=-1, keepdims=True)
    e = jnp.exp(s)
    p = e / jnp.sum(e, axis=-1, keepdims=True)
    o_ref[...] = jnp.dot(p.astype(BF16), vb, preferred_element_type=F32).astype(o_ref.dtype)


def _xattn_prompt(q, mem_k, mem_v, n_seq, *, tq):
    m, d = q.shape
    t = m // n_seq
    hd = d // MEM_HEADS
    n_mem = mem_k.shape[0] // n_seq
    nq = t // tq
    q_spec = pl.BlockSpec((tq, hd), lambda n, h, i: (n * nq + i, h))
    kv_spec = pl.BlockSpec((n_mem, hd), lambda n, h, i: (n, h))
    return pl.pallas_call(
        functools.partial(_xattn_prompt_body, scale=hd ** -0.5),
        name="xattn_prompt",
        grid=(n_seq, MEM_HEADS, nq),
        in_specs=[q_spec, kv_spec, kv_spec],
        out_specs=q_spec,
        out_shape=jax.ShapeDtypeStruct((m, d), BF16),
        compiler_params=_cparams(("parallel", "parallel", "parallel")),
    )(q, mem_k, mem_v)


def _xattn_sample_body(q_ref, k_ref, v_ref, o_ref, *, scale):
    q = q_ref[...]
    s = jnp.sum(k_ref[...] * q[None], axis=-1, keepdims=True) * scale
    s = s - jnp.max(s, axis=0, keepdims=True)
    e = jnp.exp(s)
    p = e / jnp.sum(e, axis=0, keepdims=True)
    o_ref[...] = jnp.sum(p * v_ref[...], axis=0)


def _xattn_sample(q, mem_k, mem_v, layer):
    b, heads, hd = q.shape
    n_mem = mem_k.shape[2]
    row = pl.BlockSpec((None, heads, hd), lambda n: (n, 0, 0))
    kv = pl.BlockSpec((None, None, n_mem, heads, hd), lambda n: (layer, n, 0, 0, 0))
    return pl.pallas_call(
        functools.partial(_xattn_sample_body, scale=hd ** -0.5),
        name="xattn_sample",
        grid=(b,),
        in_specs=[row, kv, kv],
        out_specs=row,
        out_shape=jax.ShapeDtypeStruct((b, heads, hd), F32),
        compiler_params=_cparams(("parallel",)),
    )(q, mem_k, mem_v)


def _layer(x, w, *, tag, tm, tn, ff2_tiles, sb_fn, s5_fn, xattn_fn, alpha):
    m, d = x.shape
    d_ssm = w['w_glu_val'].shape[0]
    d_sb = w['w_sb_branch'].shape[0]

    def mm(a, bs, name, **kw):
        return _matmul(a, bs, name=f"{tag}_{name}", tm=tm, tn=tn, **kw)

    xb = x.astype(BF16)
    w_in = w['w_in']
    u, = mm(xb, [w_in], "in_u", n=d_ssm, b_col=0, epilogue=_epi_plain, out_dtypes=[F32])
    q, = mm(xb, [w_in], "in_q", n=d_sb, b_col=d_ssm, epilogue=_epi_plain, out_dtypes=[BF16])
    k, kb = mm(xb, [w_in], "in_k", n=d_sb, b_col=d_ssm + d_sb, epilogue=_epi_two_copies, out_dtypes=[F32, BF16])
    v, vb = mm(xb, [w_in], "in_v", n=d_sb, b_col=d_ssm + 2 * d_sb, epilogue=_epi_two_copies,
               out_dtypes=[F32, BF16])
    gates, = mm(xb, [w_in], "in_gates", n=2 * d, b_col=d_ssm + 3 * d_sb, epilogue=_epi_sigmoid, out_dtypes=[F32])

    y, s5_state = s5_fn(u)
    ssm, = mm(y, [w['w_glu_val'], w['w_glu_gate']], "glu", n=d, epilogue=_epi_glu,
              extras=[(gates, 'tile', 0)], out_dtypes=[F32])
    sb = sb_fn(q, k, v, kb, vb)
    merged, = mm(sb, [w['w_sb_branch']], "merge", n=d, epilogue=_epi_merge,
                 extras=[(ssm, 'tile', 0), (gates, 'tile', d)], out_dtypes=[BF16])
    residual = functools.partial(_epi_residual, alpha=alpha)
    s1, = mm(merged, [w['w_mix_out']], "mix_out", n=d, epilogue=residual, extras=[(x, 'tile', 0)],
             out_dtypes=[F32])
    ln_tm = min(tm, 256)
    x1, x1b = _layer_norm(s1, w['ln1_g'], w['ln1_b'], name=f"{tag}_ln1", tm=ln_tm, out_dtypes=[F32, BF16])

    qm, = mm(x1b, [w['w_mem_q']], "mem_q", n=d, epilogue=_epi_plain, out_dtypes=[BF16])
    o = xattn_fn(qm)
    s2, = mm(o, [w['w_mem_o']], "mem_o", n=d, epilogue=residual, extras=[(x1, 'tile', 0)], out_dtypes=[F32])
    x2, x2b = _layer_norm(s2, w['ln2_g'], w['ln2_b'], name=f"{tag}_ln2", tm=ln_tm, out_dtypes=[F32, BF16])

    hmid, = mm(x2b, [w['w_ff1']], "ff1", n=w['w_ff1'].shape[1], epilogue=_epi_relu2, out_dtypes=[BF16])
    s3, = _matmul(hmid, [w['w_ff2_bf16']], name=f"{tag}_ff2", n=d, **ff2_tiles, epilogue=residual,
                  extras=[(x2, 'tile', 0)], out_dtypes=[F32])
    x3, = _layer_norm(s3, w['ln3_g'], w['ln3_b'], name=f"{tag}_ln3", tm=ln_tm, out_dtypes=[F32])
    return x3, k, v, s5_state


def kernel(x_prompt, x_sample, cache_k, cache_v, cache_mem_k, cache_mem_v, state_ssm_re, state_ssm_im, page_table, mem_prompt, w_in, ssm_a_re, ssm_a_im, ssm_log_dt, ssm_b_re, ssm_b_im, ssm_c_re, ssm_c_im, ssm_d, w_glu_val, w_glu_gate, sb_bias, w_sb_branch, w_mix_out, ln1_g, ln1_b, w_mem_q, w_mem_k, w_mem_v, w_mem_o, ln2_g, ln2_b, w_ff1, w_ff2, ln3_g, ln3_b):
    depth = w_in.shape[0]
    n_seq, seq, d = x_prompt.shape
    b_dec, dec_seq, _ = x_sample.shape
    assert dec_seq == 1, "the decode group is one token per sequence"
    n_mem = mem_prompt.shape[1]
    alpha = (2 * depth) ** 0.25
    heads = sb_bias.shape[1]
    groups = ssm_a_re.shape[1]

    xp = x_prompt.reshape(n_seq * seq, d)
    xs = x_sample.reshape(b_dec, d)
    memb = mem_prompt.reshape(n_seq * n_mem, d).astype(BF16)
    outs = {name: [] for name in ('kp', 'vp', 'hrp', 'hip', 'mkp', 'mvp', 'ks', 'vs', 'hrs', 'his')}
    for l in range(depth):
        w = dict(w_in=w_in[l], w_glu_val=w_glu_val[l], w_glu_gate=w_glu_gate[l], w_sb_branch=w_sb_branch[l],
                 w_mix_out=w_mix_out[l], ln1_g=ln1_g[l], ln1_b=ln1_b[l], w_mem_q=w_mem_q[l], w_mem_o=w_mem_o[l],
                 ln2_g=ln2_g[l], ln2_b=ln2_b[l], w_ff1=w_ff1[l], w_ff2_bf16=w_ff2[l].astype(BF16),
                 ln3_g=ln3_g[l], ln3_b=ln3_b[l])
        lb_re, lb_im, bb_re, bb_im = _s5_discretise(ssm_a_re[l], ssm_a_im[l], ssm_log_dt[l], ssm_b_re[l], ssm_b_im[l])
        bd, cd, lam_scan, lam_row, d_row = _s5_layout(lb_re, lb_im, bb_re, bb_im, ssm_c_re[l], ssm_c_im[l], ssm_d[l])

        mem_tiles = dict(n=d, tm=n_seq * n_mem, tn=256, epilogue=_epi_plain, out_dtypes=[F32])
        mkp, = _matmul(memb, [w_mem_k[l]], name="prompt_mem_k", **mem_tiles)
        mvp, = _matmul(memb, [w_mem_v[l]], name="prompt_mem_v", **mem_tiles)

        def s5_prompt(u):
            y, hre, him = _s5_prompt(u, n_seq, bd, cd, lam_scan, d_row)
            return y, (hre, him)

        xp, kp, vp, (hrp, hip) = _layer(
            xp, w, tag="prompt", tm=2048, tn=256, ff2_tiles=dict(tm=512, tn=256), alpha=alpha,
            s5_fn=s5_prompt,
            sb_fn=lambda q, k, v, kb, vb: _sb_prompt(q, kb, vb, sb_bias[l], n_seq, tq=256),
            xattn_fn=lambda qm: _xattn_prompt(qm, mkp, mvp, n_seq, tq=512))

        def s5_sample(u):
            y, hre, him = _s5_sample(u, state_ssm_re[l].reshape(b_dec, -1), state_ssm_im[l].reshape(b_dec, -1),
                                     bd, cd, lam_row, d_row)
            return y, (hre.reshape(b_dec, groups, -1), him.reshape(b_dec, groups, -1))

        def sb_sample(q, k, v, kb, vb):
            per_head = lambda a: a.reshape(b_dec, heads, -1)
            return _sb_sample(per_head(q), per_head(k), per_head(v), sb_bias[l], cache_k, cache_v,
                              page_table, l).reshape(b_dec, -1)

        def xattn_sample(qm):
            o = _xattn_sample(qm.astype(F32).reshape(b_dec, MEM_HEADS, -1), cache_mem_k, cache_mem_v, l)
            return o.reshape(b_dec, d).astype(BF16)

        xs, ks, vs, (hrs, his) = _layer(
            xs, w, tag="sample", tm=b_dec, tn=512, ff2_tiles=dict(tm=b_dec, tn=512), alpha=alpha,
            s5_fn=s5_sample, sb_fn=sb_sample, xattn_fn=xattn_sample)

        outs['kp'].append(kp.reshape(n_seq, seq, heads, -1))
        outs['vp'].append(vp.reshape(n_seq, seq, heads, -1))
        outs['hrp'].append(hrp)
        outs['hip'].append(hip)
        outs['mkp'].append(mkp.reshape(n_seq, n_mem, MEM_HEADS, -1))
        outs['mvp'].append(mvp.reshape(n_seq, n_mem, MEM_HEADS, -1))
        outs['ks'].append(ks.reshape(b_dec, dec_seq, heads, -1))
        outs['vs'].append(vs.reshape(b_dec, dec_seq, heads, -1))
        outs['hrs'].append(hrs)
        outs['his'].append(his)

    st = {name: jnp.stack(vals) for name, vals in outs.items()}
    return (xp.reshape(n_seq, seq, d), xs.reshape(b_dec, dec_seq, d),
            st['kp'], st['vp'], st['hrp'], st['hip'], st['mkp'], st['mvp'],
            st['ks'], st['vs'], st['hrs'], st['his'])
```

```python
import functools

import jax
import jax.numpy as jnp
from jax import lax
from jax.experimental import pallas as pl
from jax.experimental.pallas import tpu as pltpu

F32 = jnp.float32
BF16 = jnp.bfloat16

LN_EPS = 1e-5
SSM_GROUP = 16
SSM_STATE = 64
SB_HEAD_DIM = 128
MEM_HEADS = 4

LANES = 128
SUBLANES = 8
MXU_DIM = 256
VMEM_LIMIT_BYTES = 60 * 1024 * 1024

SCAN_SLABS = 8
SCAN_BLOCK_COLS = SCAN_SLABS * LANES
SCAN_BLOCK_CH = SCAN_BLOCK_COLS // SSM_STATE * SSM_GROUP
SCAN_CHUNK = 256
SCAN_PITCH = SCAN_CHUNK + SUBLANES

SB_PROMPT_HEADS_PER_STEP = 8
SB_SAMPLE_PAGES_PER_STEP = 8


def _cparams(semantics):
    return pltpu.CompilerParams(dimension_semantics=semantics, vmem_limit_bytes=VMEM_LIMIT_BYTES)


def _mm_body(*refs, n_b, n_extra, n_out, nk, epilogue, emit_a):
    a_ref = refs[0]
    b_refs = refs[1:1 + n_b]
    e_refs = refs[1 + n_b:1 + n_b + n_extra]
    o_refs = refs[1 + n_b + n_extra:1 + n_b + n_extra + n_out]
    rest = refs[1 + n_b + n_extra + n_out:]
    if emit_a:
        a_out_ref, acc_refs = rest[0], rest[1:]

        @pl.when(pl.program_id(1) == 0)
        def _():
            a_out_ref[...] = a_ref[...].astype(BF16)

        a = a_out_ref[...]
    else:
        acc_refs = rest
        a = a_ref[...]
    parts = [jnp.dot(a, b[...].astype(BF16), preferred_element_type=F32) for b in b_refs]

    def finish(accs):
        outs = epilogue(accs, [e[...] for e in e_refs])
        for o_ref, o in zip(o_refs, outs):
            o_ref[...] = o.astype(o_ref.dtype)

    if nk == 1:
        finish(parts)
    else:
        k = pl.program_id(2)

        @pl.when(k == 0)
        def _():
            for acc_ref, p in zip(acc_refs, parts):
                acc_ref[...] = p

        @pl.when(k > 0)
        def _():
            for acc_ref, p in zip(acc_refs, parts):
                acc_ref[...] += p

        @pl.when(k == nk - 1)
        def _():
            finish([acc_ref[...] for acc_ref in acc_refs])


def _matmul(a, bs, *, name, n, tm, tn, tk=None, b_col=0, epilogue, extras=(), out_dtypes, emit_a=False):
    m, k_dim = a.shape
    tk = k_dim if tk is None else tk
    assert m % tm == 0 and n % tn == 0 and k_dim % tk == 0 and b_col % tn == 0
    nk = k_dim // tk
    assert not emit_a or nk == 1
    boff = b_col // tn
    in_specs = [pl.BlockSpec((tm, tk), lambda i, j, k: (i, k))]
    for _ in bs:
        in_specs.append(pl.BlockSpec((tk, tn), lambda i, j, k: (k, j + boff)))
    operands = [a, *bs]
    for arr, kind, col in extras:
        assert col % tn == 0
        eoff = col // tn
        if kind == 'tile':
            in_specs.append(pl.BlockSpec((tm, tn), lambda i, j, k, eoff=eoff: (i, j + eoff)))
        else:
            in_specs.append(pl.BlockSpec((1, tn), lambda i, j, k, eoff=eoff: (0, j + eoff)))
        operands.append(arr)
    out_shape = [jax.ShapeDtypeStruct((m, n), dt) for dt in out_dtypes]
    out_specs = [pl.BlockSpec((tm, tn), lambda i, j, k: (i, j)) for _ in out_dtypes]
    if emit_a:
        out_shape.append(jax.ShapeDtypeStruct((m, k_dim), BF16))
        out_specs.append(pl.BlockSpec((tm, k_dim), lambda i, j, k: (i, 0)))
    scratch = [pltpu.VMEM((tm, tn), F32) for _ in bs] if nk > 1 else []
    body = functools.partial(_mm_body, n_b=len(bs), n_extra=len(extras), n_out=len(out_dtypes),
                             nk=nk, epilogue=epilogue, emit_a=emit_a)
    outs = pl.pallas_call(
        body,
        name=name,
        grid=(m // tm, n // tn, nk),
        in_specs=in_specs,
        out_specs=out_specs,
        out_shape=out_shape,
        scratch_shapes=scratch,
        compiler_params=_cparams(("parallel", "arbitrary" if emit_a else "parallel", "arbitrary")),
    )(*operands)
    return outs


def _epi_plain(accs, extras):
    return [accs[0]]


def _epi_two_copies(accs, extras):
    return [accs[0], accs[0]]


def _epi_sigmoid(accs, extras):
    return [jax.nn.sigmoid(accs[0])]


def _epi_glu(accs, extras):
    return [extras[0] * (accs[0] * jax.nn.sigmoid(accs[1]))]


def _epi_merge(accs, extras):
    return [extras[0] + extras[1] * accs[0]]


def _epi_residual(accs, extras, *, alpha):
    return [alpha * extras[0] + accs[0]]


def _epi_relu2(accs, extras):
    return [jnp.square(jnp.maximum(accs[0], 0.0))]


def _ln_body(s_ref, g_ref, b_ref, *o_refs):
    x = s_ref[...]
    mu = jnp.mean(x, axis=-1, keepdims=True)
    xc = x - mu
    var = jnp.mean(xc * xc, axis=-1, keepdims=True)
    y = xc * lax.rsqrt(var + LN_EPS) * g_ref[...] + b_ref[...]
    for o_ref in o_refs:
        o_ref[...] = y.astype(o_ref.dtype)


def _layer_norm(s, g, b, *, name, tm, out_dtypes):
    m, d = s.shape
    assert m % tm == 0
    row = pl.BlockSpec((tm, d), lambda i: (i, 0))
    vec = pl.BlockSpec((1, d), lambda i: (0, 0))
    return pl.pallas_call(
        _ln_body,
        name=name,
        grid=(m // tm,),
        in_specs=[row, vec, vec],
        out_specs=[row for _ in out_dtypes],
        out_shape=[jax.ShapeDtypeStruct((m, d), dt) for dt in out_dtypes],
        compiler_params=_cparams(("parallel",)),
    )(s, g.reshape(1, d).astype(F32), b.reshape(1, d).astype(F32))


def _s5_discretise(a_re, a_im, log_dt, b_re, b_im):
    a_re, a_im = a_re.astype(F32), a_im.astype(F32)
    b_re, b_im = b_re.astype(F32), b_im.astype(F32)
    dt = jnp.exp(log_dt.astype(F32))[:, None]
    ld_re, ld_im = a_re * dt, a_im * dt
    mag = jnp.exp(ld_re)
    lb_re, lb_im = mag * jnp.cos(ld_im), mag * jnp.sin(ld_im)
    den = a_re * a_re + a_im * a_im
    f_re = ((lb_re - 1.0) * a_re + lb_im * a_im) / den
    f_im = (lb_im * a_re - (lb_re - 1.0) * a_im) / den
    bb_re = f_re[..., None] * b_re - f_im[..., None] * b_im
    bb_im = f_re[..., None] * b_im + f_im[..., None] * b_re
    return lb_re, lb_im, bb_re, bb_im


def _s5_layout(lb_re, lb_im, bb_re, bb_im, c_re, c_im, d_skip):
    g, p, c = bb_re.shape
    nblk = g * p // SCAN_BLOCK_COLS
    gpb = g // nblk
    eye = jnp.eye(gpb, dtype=F32)

    def expand_b(bb):
        bb4 = bb.reshape(nblk, gpb, p, c)
        return jnp.einsum('jgpc,gh->jgchp', bb4, eye).reshape(nblk, gpb * c, gpb * p)

    def expand_c(cc):
        c4 = cc.astype(F32).reshape(nblk, gpb, c, p)
        return jnp.einsum('jgcp,gh->jgphc', c4, eye).reshape(nblk, gpb * p, gpb * c)

    bd = jnp.concatenate([expand_b(bb_re), expand_b(bb_im)], axis=2).astype(BF16)
    cd = jnp.concatenate([expand_c(c_re), -expand_c(c_im)], axis=1).astype(BF16)

    def scan_layout(lb):
        return lb.reshape(nblk, SCAN_SLABS, LANES).transpose(1, 0, 2)

    lam_scan = jnp.stack([scan_layout(lb_re), scan_layout(lb_im)])
    lam_row = jnp.stack([lb_re.reshape(1, g * p), lb_im.reshape(1, g * p)])
    d_row = d_skip.astype(F32).reshape(1, g * c)
    return bd, cd, lam_scan, lam_row, d_row


def _s5_prompt_body(u_ref, bd_ref, cd_ref, lam_ref, d_ref, y_ref, hre_ref, him_ref, s_re, s_im, hc):
    ti = pl.program_id(1)
    nblk = bd_ref.shape[0]
    tc = u_ref.shape[0]
    half = SCAN_BLOCK_COLS

    @pl.when(ti == 0)
    def _():
        hc[...] = jnp.zeros_like(hc)

    ub = u_ref[...].astype(BF16)
    for j in range(nblk):
        bu = jnp.dot(ub[:, j * SCAN_BLOCK_CH:(j + 1) * SCAN_BLOCK_CH], bd_ref[j], preferred_element_type=F32)
        for k in range(SCAN_SLABS):
            s_re[k, j * SCAN_PITCH:j * SCAN_PITCH + tc, :] = bu[:, k * LANES:(k + 1) * LANES]
            s_im[k, j * SCAN_PITCH:j * SCAN_PITCH + tc, :] = bu[:, half + k * LANES:half + (k + 1) * LANES]

    lr = [lam_ref[0, k] for k in range(SCAN_SLABS)]
    li = [lam_ref[1, k] for k in range(SCAN_SLABS)]

    def step(t, carry):
        hr, hi = carry
        new_r, new_i = [], []
        for k in range(SCAN_SLABS):
            rows = pl.ds(t, SUBLANES, stride=SCAN_PITCH)
            nr = lr[k] * hr[k] - li[k] * hi[k] + s_re[k, rows, :]
            ni = lr[k] * hi[k] + li[k] * hr[k] + s_im[k, rows, :]
            s_re[k, rows, :] = nr
            s_im[k, rows, :] = ni
            new_r.append(nr)
            new_i.append(ni)
        return tuple(new_r), tuple(new_i)

    init = (tuple(hc[0, k] for k in range(SCAN_SLABS)), tuple(hc[1, k] for k in range(SCAN_SLABS)))
    hr, hi = lax.fori_loop(0, tc, step, init)
    for k in range(SCAN_SLABS):
        hc[0, k] = hr[k]
        hc[1, k] = hi[k]

    for j in range(nblk):
        rows = slice(j * SCAN_PITCH, j * SCAN_PITCH + tc)
        hcat = jnp.concatenate([s_re[k, rows, :] for k in range(SCAN_SLABS)]
                               + [s_im[k, rows, :] for k in range(SCAN_SLABS)], axis=-1).astype(BF16)
        cols = slice(j * SCAN_BLOCK_CH, (j + 1) * SCAN_BLOCK_CH)
        y = jnp.dot(hcat, cd_ref[j], preferred_element_type=F32) + d_ref[:, cols] * u_ref[:, cols]
        y_ref[:, cols] = jax.nn.gelu(y).astype(y_ref.dtype)

    @pl.when(ti == pl.num_programs(1) - 1)
    def _():
        hre_ref[0] = hc[0]
        him_ref[0] = hc[1]


def _s5_prompt(u, n_seq, bd, cd, lam_scan, d_row):
    m, d_ssm = u.shape
    t = m // n_seq
    tc = SCAN_CHUNK
    assert t % tc == 0
    nblk = bd.shape[0]
    nt = t // tc
    const3 = lambda shape: pl.BlockSpec(shape, lambda n, i: (0, 0, 0), pipeline_mode=pl.Buffered(1))
    state_shape = jax.ShapeDtypeStruct((n_seq, SCAN_SLABS, SUBLANES, LANES), F32)
    state_spec = pl.BlockSpec((1, SCAN_SLABS, SUBLANES, LANES), lambda n, i: (n, 0, 0, 0))
    y, hre, him = pl.pallas_call(
        _s5_prompt_body,
        name="s5_prompt",
        grid=(n_seq, nt),
        in_specs=[
            pl.BlockSpec((tc, d_ssm), lambda n, i: (n * nt + i, 0)),
            const3(bd.shape),
            const3(cd.shape),
            pl.BlockSpec(lam_scan.shape, lambda n, i: (0, 0, 0, 0), pipeline_mode=pl.Buffered(1)),
            pl.BlockSpec(d_row.shape, lambda n, i: (0, 0)),
        ],
        out_specs=[pl.BlockSpec((tc, d_ssm), lambda n, i: (n * nt + i, 0)), state_spec, state_spec],
        out_shape=[jax.ShapeDtypeStruct((m, d_ssm), BF16), state_shape, state_shape],
        scratch_shapes=[
            pltpu.VMEM((SCAN_SLABS, nblk * SCAN_PITCH, LANES), F32),
            pltpu.VMEM((SCAN_SLABS, nblk * SCAN_PITCH, LANES), F32),
            pltpu.VMEM((2, SCAN_SLABS, SUBLANES, LANES), F32),
        ],
        compiler_params=_cparams(("arbitrary", "arbitrary")),
    )(u, bd, cd, lam_scan, d_row)

    def to_gp(h):
        return h.transpose(0, 2, 1, 3).reshape(n_seq, -1, SSM_STATE)

    return y, to_gp(hre), to_gp(him)


def _s5_sample_body(u_ref, bd_ref, cd_ref, lam_ref, d_ref, h0re_ref, h0im_ref, y_ref, hre_ref, him_ref):
    half = SCAN_BLOCK_COLS
    u = u_ref[...]
    bu = jnp.dot(u.astype(BF16), bd_ref[0], preferred_element_type=F32)
    lr, li = lam_ref[0], lam_ref[1]
    h0r, h0i = h0re_ref[...], h0im_ref[...]
    hr = bu[:, :half] + (lr * h0r - li * h0i)
    hi = bu[:, half:] + (lr * h0i + li * h0r)
    hre_ref[...] = hr
    him_ref[...] = hi
    hcat = jnp.concatenate([hr, hi], axis=-1).astype(BF16)
    y = jnp.dot(hcat, cd_ref[0], preferred_element_type=F32) + d_ref[...] * u
    y_ref[...] = jax.nn.gelu(y).astype(y_ref.dtype)


def _s5_sample(u, h0_re, h0_im, bd, cd, lam_row, d_row):
    b, d_ssm = u.shape
    nblk = bd.shape[0]
    cols = SCAN_BLOCK_COLS
    ch = SCAN_BLOCK_CH
    state = pl.BlockSpec((b, cols), lambda j: (0, j))
    return pl.pallas_call(
        _s5_sample_body,
        name="s5_sample",
        grid=(nblk,),
        in_specs=[
            pl.BlockSpec((b, ch), lambda j: (0, j)),
            pl.BlockSpec((1,) + bd.shape[1:], lambda j: (j, 0, 0)),
            pl.BlockSpec((1,) + cd.shape[1:], lambda j: (j, 0, 0)),
            pl.BlockSpec((2, 1, cols), lambda j: (0, 0, j)),
            pl.BlockSpec((1, ch), lambda j: (0, j)),
            state, state,
        ],
        out_specs=[pl.BlockSpec((b, ch), lambda j: (0, j)), state, state],
        out_shape=[jax.ShapeDtypeStruct((b, d_ssm), BF16),
                   jax.ShapeDtypeStruct(h0_re.shape, F32), jax.ShapeDtypeStruct(h0_im.shape, F32)],
        compiler_params=_cparams(("parallel",)),
    )(u, bd, cd, lam_row, d_row, h0_re, h0_im)


def _log_keep(z):
    return -(jnp.maximum(z, 0.0) + jnp.log(1.0 + jnp.exp(-jnp.abs(z))))


def _split_bf16(x):
    hi = x.astype(BF16)
    lo = (x - hi.astype(F32)).astype(BF16)
    return jnp.concatenate([hi, lo], axis=-1)


def _sb_prompt_body(bias_ref, q_ref, k_ref, v_ref, o_ref, *, scale, heads_per_step):
    hp = pl.program_id(1)
    qi = pl.program_id(2)
    tq = q_ref.shape[0]
    dh = SB_HEAD_DIM
    r = lax.broadcasted_iota(jnp.int32, (tq, tq), 0)
    c = lax.broadcasted_iota(jnp.int32, (tq, tq), 1)
    below = r > c
    tri = jnp.where(below, 1.0, 0.0).astype(BF16)
    tri2 = jnp.concatenate([tri, tri], axis=0)
    nh = heads_per_step
    qs = [q_ref[:, hh * dh:(hh + 1) * dh] for hh in range(nh)]
    biases = [bias_ref[hp * nh + hh] for hh in range(nh)]

    def scores(kb):
        rows = pl.ds(pl.multiple_of(kb * tq, tq), tq)
        return tuple(lax.dot_general(qs[hh], k_ref[rows, hh * dh:(hh + 1) * dh], (((1,), (1,)), ((), ())),
                                     preferred_element_type=F32) for hh in range(nh))

    def block(kb, zs, carry, diagonal):
        rows = pl.ds(pl.multiple_of(kb * tq, tq), tq)
        z = [zs[hh] * scale + biases[hh] for hh in range(nh)]
        lk = [_log_keep(z[hh]) for hh in range(nh)]
        log_beta = [z[hh] + lk[hh] for hh in range(nh)]
        if diagonal:
            lk = [jnp.where(below, lk[hh], 0.0) for hh in range(nh)]
        after = [jnp.dot(_split_bf16(lk[hh]), tri2, preferred_element_type=F32) for hh in range(nh)]
        w = [jnp.exp(log_beta[hh] + after[hh] + carry[hh][1]) for hh in range(nh)]
        if diagonal:
            w = [jnp.where(below, w[hh], 0.0) for hh in range(nh)]
        new = []
        for hh in range(nh):
            acc = carry[hh][0] + jnp.dot(w[hh].astype(BF16), v_ref[rows, hh * dh:(hh + 1) * dh],
                                         preferred_element_type=F32)
            run = carry[hh][1] + jnp.sum(lk[hh], axis=-1, keepdims=True)
            new.append((acc, run))
        return tuple(new)

    zero = tuple((jnp.zeros((tq, dh), F32), jnp.zeros((tq, 1), F32)) for _ in range(nh))
    carry = block(qi, scores(qi), zero, True)
    carry = lax.fori_loop(0, qi, lambda i, cr: block(qi - 1 - i, scores(qi - 1 - i), cr, False), carry)
    for hh in range(nh):
        o_ref[:, hh * dh:(hh + 1) * dh] = carry[hh][0].astype(o_ref.dtype)


def _sb_prompt(q, k, v, bias, n_seq, *, tq):
    m, hd = q.shape
    t = m // n_seq
    hps = SB_PROMPT_HEADS_PER_STEP
    heads = hd // SB_HEAD_DIM
    assert t % tq == 0 and heads % hps == 0
    nq = t // tq
    kv_spec = pl.BlockSpec((t, hps * SB_HEAD_DIM), lambda n, h, i: (n, h))
    q_spec = pl.BlockSpec((tq, hps * SB_HEAD_DIM), lambda n, h, i: (n * nq + i, h))
    return pl.pallas_call(
        functools.partial(_sb_prompt_body, scale=SB_HEAD_DIM ** -0.5, heads_per_step=hps),
        name="sb_prompt",
        grid=(n_seq, heads // hps, nq),
        in_specs=[pl.BlockSpec(memory_space=pltpu.SMEM), q_spec, kv_spec, kv_spec],
        out_specs=q_spec,
        out_shape=jax.ShapeDtypeStruct((m, hd), BF16),
        compiler_params=_cparams(("parallel", "parallel", "arbitrary")),
    )(bias.astype(F32), q, k, v)


def _sb_sample_body(pt_ref, bias_ref, q_ref, knew_ref, vnew_ref, *refs, scale, n_pp, new_key_visible):
    k_refs, v_refs = refs[:n_pp], refs[n_pp:2 * n_pp]
    o_ref, acc, run = refs[2 * n_pp:]
    p = pl.program_id(1)
    heads, dh = q_ref.shape
    page = k_refs[0].shape[0]
    ncol = page * heads
    tile = MXU_DIM
    bias = bias_ref[...]
    q = q_ref[...]

    row = lax.broadcasted_iota(jnp.int32, (heads, ncol), 0)
    col = lax.broadcasted_iota(jnp.int32, (heads, ncol), 1)
    own = lax.rem(col, jnp.full_like(col, heads)) == row
    tr = lax.broadcasted_iota(jnp.int32, (tile, tile), 0)
    tc = lax.broadcasted_iota(jnp.int32, (tile, tile), 1)
    later = lax.div(tr, jnp.full_like(tr, heads)) > lax.div(tc, jnp.full_like(tc, heads))
    tri = jnp.where(later, 1.0, 0.0).astype(BF16)
    tri2 = jnp.concatenate([tri, tri], axis=0)

    @pl.when(p == 0)
    def _():
        z_new = jnp.sum(q.astype(F32) * knew_ref[...], axis=-1, keepdims=True) * scale + bias
        lk_new = _log_keep(z_new)
        w_new = jnp.exp(z_new + lk_new)
        if not new_key_visible:
            lk_new = jnp.zeros_like(lk_new)
            w_new = jnp.zeros_like(w_new)
        run[...] = lk_new
        acc[...] = w_new * vnew_ref[...]

    nt = ncol // tile
    v2 = [v_refs[i][...].reshape(ncol, dh).astype(BF16) for i in range(n_pp)]
    z = [lax.dot_general(q, k_refs[i][...].reshape(ncol, dh).astype(BF16), (((1,), (1,)), ((), ())),
                         preferred_element_type=F32) * scale + bias for i in range(n_pp)]
    lk_raw = [_log_keep(z[i]) for i in range(n_pp)]
    log_beta = [z[i] + lk_raw[i] for i in range(n_pp)]
    lk = [jnp.where(own, lk_raw[i], 0.0) for i in range(n_pp)]
    after = [[jnp.dot(_split_bf16(lk[i][:, t * tile:(t + 1) * tile]), tri2, preferred_element_type=F32)
              for t in range(nt)] for i in range(n_pp)]
    total = [[jnp.sum(lk[i][:, t * tile:(t + 1) * tile], axis=-1, keepdims=True) for t in range(nt)]
             for i in range(n_pp)]
    carry = run[...]
    out = acc[...]
    for i in range(n_pp):
        ws = [None] * nt
        for t in reversed(range(nt)):
            cols = slice(t * tile, (t + 1) * tile)
            ws[t] = jnp.exp(log_beta[i][:, cols] + after[i][t] + carry)
            carry = carry + total[i][t]
        w = jnp.where(own, jnp.concatenate(ws, axis=-1), 0.0)
        out = out + jnp.dot(w.astype(BF16), v2[i], preferred_element_type=F32)
    run[...] = carry
    acc[...] = out

    @pl.when(p == pl.num_programs(1) - 1)
    def _():
        o_ref[...] = out.astype(o_ref.dtype)


def _sb_sample(q, k_new, v_new, bias, pool_k, pool_v, page_table, layer):
    b, heads, dh = q.shape
    page = pool_k.shape[2]
    n_pages = page_table.shape[1]
    n_pp = SB_SAMPLE_PAGES_PER_STEP
    assert n_pages % n_pp == 0 and (page * heads) % MXU_DIM == 0 and MXU_DIM % heads == 0
    past = n_pages * page
    new_key_pos, query_pos = past, past
    new_key_visible = new_key_pos < query_pos
    row = pl.BlockSpec((None, heads, dh), lambda n, p, pt: (n, 0, 0))

    def page_spec(i):
        return pl.BlockSpec(
            (None, None, page, heads, dh),
            lambda n, p, pt: (layer, pt[n * n_pages + n_pages - 1 - (p * n_pp + i)], 0, 0, 0))

    grid_spec = pltpu.PrefetchScalarGridSpec(
        num_scalar_prefetch=1,
        grid=(b, n_pages // n_pp),
        in_specs=[pl.BlockSpec((heads, 1), lambda n, p, pt: (0, 0)), row, row, row]
                 + [page_spec(i) for i in range(n_pp)] * 2,
        out_specs=row,
        scratch_shapes=[pltpu.VMEM((heads, dh), F32), pltpu.VMEM((heads, 1), F32)],
    )
    return pl.pallas_call(
        functools.partial(_sb_sample_body, scale=SB_HEAD_DIM ** -0.5, n_pp=n_pp, new_key_visible=new_key_visible),
        name="sb_sample",
        grid_spec=grid_spec,
        out_shape=jax.ShapeDtypeStruct((b, heads, dh), BF16),
        compiler_params=_cparams(("parallel", "arbitrary")),
    )(page_table.reshape(-1).astype(jnp.int32), bias.astype(F32).reshape(heads, 1), q, k_new, v_new,
      *([pool_k] * n_pp), *([pool_v] * n_pp))


def _xattn_prompt_body(q_ref, k_ref, v_ref, o_ref, *, scale):
    q = q_ref[...]
    kb = k_ref[...].astype(BF16)
    vb = v_ref[...].astype(BF16)
    s = lax.dot_general(q, kb, (((1,), (1,)), ((), ())), preferred_element_type=F32) * scale
    s = s - jnp.max(s, axis=-1, keepdims=True)
    e = jnp.exp(s)
    p = e / jnp.sum(e, axis=-1, keepdims=True)
    o_ref[...] = jnp.dot(p.astype(BF16), vb, preferred_element_type=F32).astype(o_ref.dtype)


def _xattn_prompt(q, mem_k, mem_v, n_seq, *, tq):
    m, d = q.shape
    t = m // n_seq
    hd = d // MEM_HEADS
    n_mem = mem_k.shape[0] // n_seq
    nq = t // tq
    q_spec = pl.BlockSpec((tq, hd), lambda n, h, i: (n * nq + i, h))
    kv_spec = pl.BlockSpec((n_mem, hd), lambda n, h, i: (n, h))
    return pl.pallas_call(
        functools.partial(_xattn_prompt_body, scale=hd ** -0.5),
        name="xattn_prompt",
        grid=(n_seq, MEM_HEADS, nq),
        in_specs=[q_spec, kv_spec, kv_spec],
        out_specs=q_spec,
        out_shape=jax.ShapeDtypeStruct((m, d), BF16),
        compiler_params=_cparams(("parallel", "parallel", "parallel")),
    )(q, mem_k, mem_v)


def _xattn_sample_body(q_ref, k_ref, v_ref, o_ref, *, scale):
    q = q_ref[...]
    s = jnp.sum(k_ref[...] * q[None], axis=-1, keepdims=True) * scale
    s = s - jnp.max(s, axis=0, keepdims=True)
    e = jnp.exp(s)
    p = e / jnp.sum(e, axis=0, keepdims=True)
    o_ref[...] = jnp.sum(p * v_ref[...], axis=0)


def _xattn_sample(q, mem_k, mem_v, layer):
    b, heads, hd = q.shape
    n_mem = mem_k.shape[2]
    row = pl.BlockSpec((None, heads, hd), lambda n: (n, 0, 0))
    kv = pl.BlockSpec((None, None, n_mem, heads, hd), lambda n: (layer, n, 0, 0, 0))
    return pl.pallas_call(
        functools.partial(_xattn_sample_body, scale=hd ** -0.5),
        name="xattn_sample",
        grid=(b,),
        in_specs=[row, kv, kv],
        out_specs=row,
        out_shape=jax.ShapeDtypeStruct((b, heads, hd), F32),
        compiler_params=_cparams(("parallel",)),
    )(q, mem_k, mem_v)


def _layer(x, w, *, tag, tm, tn, first_tiles, wide_tiles, ff2_tiles, sb_fn, s5_fn, xattn_fn, alpha):
    m, d = x.shape
    d_ssm = w['w_glu_val'].shape[0]
    d_sb = w['w_sb_branch'].shape[0]

    def mm(a, bs, name, tiles=None, **kw):
        return _matmul(a, bs, name=f"{tag}_{name}", **(tiles or dict(tm=tm, tn=tn)), **kw)

    w_in = w['w_in']
    u, xb = mm(x, [w_in], "in_u", tiles=first_tiles, n=d_ssm, b_col=0, epilogue=_epi_plain, out_dtypes=[F32],
               emit_a=True)
    q, = mm(xb, [w_in], "in_q", n=d_sb, b_col=d_ssm, epilogue=_epi_plain, out_dtypes=[BF16])
    k, kb = mm(xb, [w_in], "in_k", n=d_sb, b_col=d_ssm + d_sb, epilogue=_epi_two_copies, out_dtypes=[F32, BF16])
    v, vb = mm(xb, [w_in], "in_v", n=d_sb, b_col=d_ssm + 2 * d_sb, epilogue=_epi_two_copies,
               out_dtypes=[F32, BF16])
    gates, = mm(xb, [w_in], "in_gates", tiles=wide_tiles, n=2 * d, b_col=d_ssm + 3 * d_sb, epilogue=_epi_sigmoid,
                out_dtypes=[F32])

    y, s5_state = s5_fn(u)
    ssm, = mm(y, [w['w_glu_val'], w['w_glu_gate']], "glu", n=d, epilogue=_epi_glu,
              extras=[(gates, 'tile', 0)], out_dtypes=[F32])
    sb = sb_fn(q, k, v, kb, vb)
    merged, = mm(sb, [w['w_sb_branch']], "merge", n=d, epilogue=_epi_merge,
                 extras=[(ssm, 'tile', 0), (gates, 'tile', d)], out_dtypes=[BF16])
    residual = functools.partial(_epi_residual, alpha=alpha)
    s1, = mm(merged, [w['w_mix_out']], "mix_out", n=d, epilogue=residual, extras=[(x, 'tile', 0)],
             out_dtypes=[F32])
    ln_tm = min(tm, 256)
    x1, x1b = _layer_norm(s1, w['ln1_g'], w['ln1_b'], name=f"{tag}_ln1", tm=ln_tm, out_dtypes=[F32, BF16])

    qm, = mm(x1b, [w['w_mem_q']], "mem_q", n=d, epilogue=_epi_plain, out_dtypes=[BF16])
    o = xattn_fn(qm)
    s2, = mm(o, [w['w_mem_o']], "mem_o", n=d, epilogue=residual, extras=[(x1, 'tile', 0)], out_dtypes=[F32])
    x2, x2b = _layer_norm(s2, w['ln2_g'], w['ln2_b'], name=f"{tag}_ln2", tm=ln_tm, out_dtypes=[F32, BF16])

    hmid, = mm(x2b, [w['w_ff1']], "ff1", tiles=wide_tiles, n=w['w_ff1'].shape[1], epilogue=_epi_relu2,
               out_dtypes=[BF16])
    s3, = _matmul(hmid, [w['w_ff2_bf16']], name=f"{tag}_ff2", n=d, **ff2_tiles, epilogue=residual,
                  extras=[(x2, 'tile', 0)], out_dtypes=[F32])
    x3, = _layer_norm(s3, w['ln3_g'], w['ln3_b'], name=f"{tag}_ln3", tm=ln_tm, out_dtypes=[F32])
    return x3, k, v, s5_state


def kernel(x_prompt, x_sample, cache_k, cache_v, cache_mem_k, cache_mem_v, state_ssm_re, state_ssm_im, page_table, mem_prompt, w_in, ssm_a_re, ssm_a_im, ssm_log_dt, ssm_b_re, ssm_b_im, ssm_c_re, ssm_c_im, ssm_d, w_glu_val, w_glu_gate, sb_bias, w_sb_branch, w_mix_out, ln1_g, ln1_b, w_mem_q, w_mem_k, w_mem_v, w_mem_o, ln2_g, ln2_b, w_ff1, w_ff2, ln3_g, ln3_b):
    depth = w_in.shape[0]
    n_seq, seq, d = x_prompt.shape
    b_dec, dec_seq, _ = x_sample.shape
    assert dec_seq == 1, "the decode group is one token per sequence"
    n_mem = mem_prompt.shape[1]
    alpha = (2 * depth) ** 0.25
    heads = sb_bias.shape[1]
    groups = ssm_a_re.shape[1]

    xp = x_prompt.reshape(n_seq * seq, d)
    xs = x_sample.reshape(b_dec, d)
    memb = mem_prompt.reshape(n_seq * n_mem, d).astype(BF16)
    outs = {name: [] for name in ('kp', 'vp', 'hrp', 'hip', 'mkp', 'mvp', 'ks', 'vs', 'hrs', 'his')}
    for l in range(depth):
        w = dict(w_in=w_in[l], w_glu_val=w_glu_val[l], w_glu_gate=w_glu_gate[l], w_sb_branch=w_sb_branch[l],
                 w_mix_out=w_mix_out[l], ln1_g=ln1_g[l], ln1_b=ln1_b[l], w_mem_q=w_mem_q[l], w_mem_o=w_mem_o[l],
                 ln2_g=ln2_g[l], ln2_b=ln2_b[l], w_ff1=w_ff1[l], w_ff2_bf16=w_ff2[l].astype(BF16),
                 ln3_g=ln3_g[l], ln3_b=ln3_b[l])
        lb_re, lb_im, bb_re, bb_im = _s5_discretise(ssm_a_re[l], ssm_a_im[l], ssm_log_dt[l], ssm_b_re[l], ssm_b_im[l])
        bd, cd, lam_scan, lam_row, d_row = _s5_layout(lb_re, lb_im, bb_re, bb_im, ssm_c_re[l], ssm_c_im[l], ssm_d[l])

        mem_tiles = dict(n=d, tm=n_seq * n_mem, tn=256, epilogue=_epi_plain, out_dtypes=[F32])
        mkp, = _matmul(memb, [w_mem_k[l]], name="prompt_mem_k", **mem_tiles)
        mvp, = _matmul(memb, [w_mem_v[l]], name="prompt_mem_v", **mem_tiles)

        def s5_prompt(u):
            y, hre, him = _s5_prompt(u, n_seq, bd, cd, lam_scan, d_row)
            return y, (hre, him)

        xp, kp, vp, (hrp, hip) = _layer(
            xp, w, tag="prompt", tm=2048, tn=256, first_tiles=dict(tm=512, tn=256),
            wide_tiles=dict(tm=1024, tn=512), ff2_tiles=dict(tm=512, tn=256), alpha=alpha, s5_fn=s5_prompt,
            sb_fn=lambda q, k, v, kb, vb: _sb_prompt(q, kb, vb, sb_bias[l], n_seq, tq=256),
            xattn_fn=lambda qm: _xattn_prompt(qm, mkp, mvp, n_seq, tq=512))

        def s5_sample(u):
            y, hre, him = _s5_sample(u, state_ssm_re[l].reshape(b_dec, -1), state_ssm_im[l].reshape(b_dec, -1),
                                     bd, cd, lam_row, d_row)
            return y, (hre.reshape(b_dec, groups, -1), him.reshape(b_dec, groups, -1))

        def sb_sample(q, k, v, kb, vb):
            per_head = lambda a: a.reshape(b_dec, heads, -1)
            return _sb_sample(per_head(q), per_head(k), per_head(v), sb_bias[l], cache_k, cache_v,
                              page_table, l).reshape(b_dec, -1)

        def xattn_sample(qm):
            o = _xattn_sample(qm.astype(F32).reshape(b_dec, MEM_HEADS, -1), cache_mem_k, cache_mem_v, l)
            return o.reshape(b_dec, d).astype(BF16)

        xs, ks, vs, (hrs, his) = _layer(
            xs, w, tag="sample", tm=b_dec, tn=512, first_tiles=dict(tm=b_dec, tn=512),
            wide_tiles=dict(tm=b_dec, tn=512), ff2_tiles=dict(tm=b_dec, tn=512), alpha=alpha,
            s5_fn=s5_sample, sb_fn=sb_sample, xattn_fn=xattn_sample)

        outs['kp'].append(kp.reshape(n_seq, seq, heads, -1))
        outs['vp'].append(vp.reshape(n_seq, seq, heads, -1))
        outs['hrp'].append(hrp)
        outs['hip'].append(hip)
        outs['mkp'].append(mkp.reshape(n_seq, n_mem, MEM_HEADS, -1))
        outs['mvp'].append(mvp.reshape(n_seq, n_mem, MEM_HEADS, -1))
        outs['ks'].append(ks.reshape(b_dec, dec_seq, heads, -1))
        outs['vs'].append(vs.reshape(b_dec, dec_seq, heads, -1))
        outs['hrs'].append(hrs)
        outs['his'].append(his)

    st = {name: jnp.stack(vals) for name, vals in outs.items()}
    return (xp.reshape(n_seq, seq, d), xs.reshape(b_dec, dec_seq, d),
            st['kp'], st['vp'], st['hrp'], st['hip'], st['mkp'], st['mvp'],
            st['ks'], st['vs'], st['hrs'], st['his'])
```

```python
import functools

import jax
import jax.numpy as jnp
from jax import lax
from jax.experimental import pallas as pl
from jax.experimental.pallas import tpu as pltpu

F32 = jnp.float32
BF16 = jnp.bfloat16

LN_EPS = 1e-5
SSM_GROUP = 16
SSM_STATE = 64
SB_HEAD_DIM = 128
MEM_HEADS = 4

LANES = 128
SUBLANES = 8
MXU_DIM = 256
VMEM_LIMIT_BYTES = 60 * 1024 * 1024

SCAN_SLABS = 8
SCAN_BLOCK_COLS = SCAN_SLABS * LANES
SCAN_BLOCK_CH = SCAN_BLOCK_COLS // SSM_STATE * SSM_GROUP
SCAN_CHUNK = 256
SCAN_PITCH = SCAN_CHUNK + SUBLANES

SB_PROMPT_HEADS_PER_STEP = 8
SB_SAMPLE_PAGES_PER_STEP = 8
XATTN_SAMPLE_SEQS_PER_STEP = 2


def _cparams(semantics):
    return pltpu.CompilerParams(dimension_semantics=semantics, vmem_limit_bytes=VMEM_LIMIT_BYTES)


def _mm_body(*refs, n_b, n_extra, n_out, nk, epilogue, emit_a):
    a_ref = refs[0]
    b_refs = refs[1:1 + n_b]
    e_refs = refs[1 + n_b:1 + n_b + n_extra]
    o_refs = refs[1 + n_b + n_extra:1 + n_b + n_extra + n_out]
    rest = refs[1 + n_b + n_extra + n_out:]
    if emit_a:
        a_out_ref, acc_refs = rest[0], rest[1:]

        @pl.when(pl.program_id(1) == 0)
        def _():
            a_out_ref[...] = a_ref[...].astype(BF16)

        a = a_out_ref[...]
    else:
        acc_refs = rest
        a = a_ref[...]
    parts = [jnp.dot(a, b[...].astype(BF16), preferred_element_type=F32) for b in b_refs]

    def finish(accs):
        outs = epilogue(accs, [e[...] for e in e_refs])
        for o_ref, o in zip(o_refs, outs):
            o_ref[...] = o.astype(o_ref.dtype)

    if nk == 1:
        finish(parts)
    else:
        k = pl.program_id(2)

        @pl.when(k == 0)
        def _():
            for acc_ref, p in zip(acc_refs, parts):
                acc_ref[...] = p

        @pl.when(k > 0)
        def _():
            for acc_ref, p in zip(acc_refs, parts):
                acc_ref[...] += p

        @pl.when(k == nk - 1)
        def _():
            finish([acc_ref[...] for acc_ref in acc_refs])


def _matmul(a, bs, *, name, n, tm, tn, tk=None, b_col=0, epilogue, extras=(), out_dtypes, emit_a=False):
    m, k_dim = a.shape
    tk = k_dim if tk is None else tk
    assert m % tm == 0 and n % tn == 0 and k_dim % tk == 0 and b_col % tn == 0
    nk = k_dim // tk
    assert not emit_a or nk == 1
    boff = b_col // tn
    in_specs = [pl.BlockSpec((tm, tk), lambda i, j, k: (i, k))]
    for _ in bs:
        in_specs.append(pl.BlockSpec((tk, tn), lambda i, j, k: (k, j + boff)))
    operands = [a, *bs]
    for arr, kind, col in extras:
        assert col % tn == 0
        eoff = col // tn
        if kind == 'tile':
            in_specs.append(pl.BlockSpec((tm, tn), lambda i, j, k, eoff=eoff: (i, j + eoff)))
        else:
            in_specs.append(pl.BlockSpec((1, tn), lambda i, j, k, eoff=eoff: (0, j + eoff)))
        operands.append(arr)
    out_shape = [jax.ShapeDtypeStruct((m, n), dt) for dt in out_dtypes]
    out_specs = [pl.BlockSpec((tm, tn), lambda i, j, k: (i, j)) for _ in out_dtypes]
    if emit_a:
        out_shape.append(jax.ShapeDtypeStruct((m, k_dim), BF16))
        out_specs.append(pl.BlockSpec((tm, k_dim), lambda i, j, k: (i, 0)))
    scratch = [pltpu.VMEM((tm, tn), F32) for _ in bs] if nk > 1 else []
    body = functools.partial(_mm_body, n_b=len(bs), n_extra=len(extras), n_out=len(out_dtypes),
                             nk=nk, epilogue=epilogue, emit_a=emit_a)
    outs = pl.pallas_call(
        body,
        name=name,
        grid=(m // tm, n // tn, nk),
        in_specs=in_specs,
        out_specs=out_specs,
        out_shape=out_shape,
        scratch_shapes=scratch,
        compiler_params=_cparams(("parallel", "arbitrary" if emit_a else "parallel", "arbitrary")),
    )(*operands)
    return outs


def _epi_plain(accs, extras):
    return [accs[0]]


def _epi_two_copies(accs, extras):
    return [accs[0], accs[0]]


def _epi_sigmoid(accs, extras):
    return [jax.nn.sigmoid(accs[0])]


def _epi_glu(accs, extras):
    return [extras[0] * (accs[0] * jax.nn.sigmoid(accs[1]))]


def _epi_merge(accs, extras):
    return [extras[0] + extras[1] * accs[0]]


def _epi_residual(accs, extras, *, alpha):
    return [alpha * extras[0] + accs[0]]


def _epi_relu2(accs, extras):
    return [jnp.square(jnp.maximum(accs[0], 0.0))]


def _ln_body(s_ref, g_ref, b_ref, *o_refs):
    x = s_ref[...]
    mu = jnp.mean(x, axis=-1, keepdims=True)
    xc = x - mu
    var = jnp.mean(xc * xc, axis=-1, keepdims=True)
    y = xc * lax.rsqrt(var + LN_EPS) * g_ref[...] + b_ref[...]
    for o_ref in o_refs:
        o_ref[...] = y.astype(o_ref.dtype)


def _layer_norm(s, g, b, *, name, tm, out_dtypes):
    m, d = s.shape
    assert m % tm == 0
    row = pl.BlockSpec((tm, d), lambda i: (i, 0))
    vec = pl.BlockSpec((1, d), lambda i: (0, 0))
    return pl.pallas_call(
        _ln_body,
        name=name,
        grid=(m // tm,),
        in_specs=[row, vec, vec],
        out_specs=[row for _ in out_dtypes],
        out_shape=[jax.ShapeDtypeStruct((m, d), dt) for dt in out_dtypes],
        compiler_params=_cparams(("parallel",)),
    )(s, g.reshape(1, d).astype(F32), b.reshape(1, d).astype(F32))


def _s5_discretise(a_re, a_im, log_dt, b_re, b_im):
    a_re, a_im = a_re.astype(F32), a_im.astype(F32)
    b_re, b_im = b_re.astype(F32), b_im.astype(F32)
    dt = jnp.exp(log_dt.astype(F32))[:, None]
    ld_re, ld_im = a_re * dt, a_im * dt
    mag = jnp.exp(ld_re)
    lb_re, lb_im = mag * jnp.cos(ld_im), mag * jnp.sin(ld_im)
    den = a_re * a_re + a_im * a_im
    f_re = ((lb_re - 1.0) * a_re + lb_im * a_im) / den
    f_im = (lb_im * a_re - (lb_re - 1.0) * a_im) / den
    bb_re = f_re[..., None] * b_re - f_im[..., None] * b_im
    bb_im = f_re[..., None] * b_im + f_im[..., None] * b_re
    return lb_re, lb_im, bb_re, bb_im


def _s5_layout(lb_re, lb_im, bb_re, bb_im, c_re, c_im, d_skip):
    g, p, c = bb_re.shape
    nblk = g * p // SCAN_BLOCK_COLS
    gpb = g // nblk
    eye = jnp.eye(gpb, dtype=F32)

    def expand_b(bb):
        bb4 = bb.reshape(nblk, gpb, p, c)
        return jnp.einsum('jgpc,gh->jgchp', bb4, eye).reshape(nblk, gpb * c, gpb * p)

    def expand_c(cc):
        c4 = cc.astype(F32).reshape(nblk, gpb, c, p)
        return jnp.einsum('jgcp,gh->jgphc', c4, eye).reshape(nblk, gpb * p, gpb * c)

    bd = jnp.concatenate([expand_b(bb_re), expand_b(bb_im)], axis=2).astype(BF16)
    cd = jnp.concatenate([expand_c(c_re), -expand_c(c_im)], axis=1).astype(BF16)

    def scan_layout(lb):
        return lb.reshape(nblk, SCAN_SLABS, LANES).transpose(1, 0, 2)

    lam_scan = jnp.stack([scan_layout(lb_re), scan_layout(lb_im)])
    lam_row = jnp.stack([lb_re.reshape(1, g * p), lb_im.reshape(1, g * p)])
    d_row = d_skip.astype(F32).reshape(1, g * c)
    return bd, cd, lam_scan, lam_row, d_row


def _s5_prompt_body(u_ref, bd_ref, cd_ref, lam_ref, d_ref, y_ref, hre_ref, him_ref, s_re, s_im, hc):
    ti = pl.program_id(1)
    nblk = bd_ref.shape[0]
    tc = u_ref.shape[0]
    half = SCAN_BLOCK_COLS

    @pl.when(ti == 0)
    def _():
        hc[...] = jnp.zeros_like(hc)

    ub = u_ref[...].astype(BF16)
    for j in range(nblk):
        bu = jnp.dot(ub[:, j * SCAN_BLOCK_CH:(j + 1) * SCAN_BLOCK_CH], bd_ref[j], preferred_element_type=F32)
        for k in range(SCAN_SLABS):
            s_re[k, j * SCAN_PITCH:j * SCAN_PITCH + tc, :] = bu[:, k * LANES:(k + 1) * LANES]
            s_im[k, j * SCAN_PITCH:j * SCAN_PITCH + tc, :] = bu[:, half + k * LANES:half + (k + 1) * LANES]

    lr = [lam_ref[0, k] for k in range(SCAN_SLABS)]
    li = [lam_ref[1, k] for k in range(SCAN_SLABS)]

    def step(t, carry):
        hr, hi = carry
        new_r, new_i = [], []
        for k in range(SCAN_SLABS):
            rows = pl.ds(t, SUBLANES, stride=SCAN_PITCH)
            nr = lr[k] * hr[k] - li[k] * hi[k] + s_re[k, rows, :]
            ni = lr[k] * hi[k] + li[k] * hr[k] + s_im[k, rows, :]
            s_re[k, rows, :] = nr
            s_im[k, rows, :] = ni
            new_r.append(nr)
            new_i.append(ni)
        return tuple(new_r), tuple(new_i)

    init = (tuple(hc[0, k] for k in range(SCAN_SLABS)), tuple(hc[1, k] for k in range(SCAN_SLABS)))
    hr, hi = lax.fori_loop(0, tc, step, init)
    for k in range(SCAN_SLABS):
        hc[0, k] = hr[k]
        hc[1, k] = hi[k]

    for j in range(nblk):
        rows = slice(j * SCAN_PITCH, j * SCAN_PITCH + tc)
        hcat = jnp.concatenate([s_re[k, rows, :] for k in range(SCAN_SLABS)]
                               + [s_im[k, rows, :] for k in range(SCAN_SLABS)], axis=-1).astype(BF16)
        cols = slice(j * SCAN_BLOCK_CH, (j + 1) * SCAN_BLOCK_CH)
        y = jnp.dot(hcat, cd_ref[j], preferred_element_type=F32) + d_ref[:, cols] * u_ref[:, cols]
        y_ref[:, cols] = jax.nn.gelu(y).astype(y_ref.dtype)

    @pl.when(ti == pl.num_programs(1) - 1)
    def _():
        hre_ref[0] = hc[0]
        him_ref[0] = hc[1]


def _s5_prompt(u, n_seq, bd, cd, lam_scan, d_row):
    m, d_ssm = u.shape
    t = m // n_seq
    tc = SCAN_CHUNK
    assert t % tc == 0
    nblk = bd.shape[0]
    nt = t // tc
    const3 = lambda shape: pl.BlockSpec(shape, lambda n, i: (0, 0, 0), pipeline_mode=pl.Buffered(1))
    state_shape = jax.ShapeDtypeStruct((n_seq, SCAN_SLABS, SUBLANES, LANES), F32)
    state_spec = pl.BlockSpec((1, SCAN_SLABS, SUBLANES, LANES), lambda n, i: (n, 0, 0, 0))
    y, hre, him = pl.pallas_call(
        _s5_prompt_body,
        name="s5_prompt",
        grid=(n_seq, nt),
        in_specs=[
            pl.BlockSpec((tc, d_ssm), lambda n, i: (n * nt + i, 0)),
            const3(bd.shape),
            const3(cd.shape),
            pl.BlockSpec(lam_scan.shape, lambda n, i: (0, 0, 0, 0), pipeline_mode=pl.Buffered(1)),
            pl.BlockSpec(d_row.shape, lambda n, i: (0, 0)),
        ],
        out_specs=[pl.BlockSpec((tc, d_ssm), lambda n, i: (n * nt + i, 0)), state_spec, state_spec],
        out_shape=[jax.ShapeDtypeStruct((m, d_ssm), BF16), state_shape, state_shape],
        scratch_shapes=[
            pltpu.VMEM((SCAN_SLABS, nblk * SCAN_PITCH, LANES), F32),
            pltpu.VMEM((SCAN_SLABS, nblk * SCAN_PITCH, LANES), F32),
            pltpu.VMEM((2, SCAN_SLABS, SUBLANES, LANES), F32),
        ],
        compiler_params=_cparams(("arbitrary", "arbitrary")),
    )(u, bd, cd, lam_scan, d_row)

    def to_gp(h):
        return h.transpose(0, 2, 1, 3).reshape(n_seq, -1, SSM_STATE)

    return y, to_gp(hre), to_gp(him)


def _s5_sample_body(u_ref, bd_ref, cd_ref, lam_ref, d_ref, h0re_ref, h0im_ref, y_ref, hre_ref, him_ref):
    half = SCAN_BLOCK_COLS
    u = u_ref[...]
    bu = jnp.dot(u.astype(BF16), bd_ref[0], preferred_element_type=F32)
    lr, li = lam_ref[0], lam_ref[1]
    h0r, h0i = h0re_ref[...], h0im_ref[...]
    hr = bu[:, :half] + (lr * h0r - li * h0i)
    hi = bu[:, half:] + (lr * h0i + li * h0r)
    hre_ref[...] = hr
    him_ref[...] = hi
    hcat = jnp.concatenate([hr, hi], axis=-1).astype(BF16)
    y = jnp.dot(hcat, cd_ref[0], preferred_element_type=F32) + d_ref[...] * u
    y_ref[...] = jax.nn.gelu(y).astype(y_ref.dtype)


def _s5_sample(u, h0_re, h0_im, bd, cd, lam_row, d_row):
    b, d_ssm = u.shape
    nblk = bd.shape[0]
    cols = SCAN_BLOCK_COLS
    ch = SCAN_BLOCK_CH
    state = pl.BlockSpec((b, cols), lambda j: (0, j))
    return pl.pallas_call(
        _s5_sample_body,
        name="s5_sample",
        grid=(nblk,),
        in_specs=[
            pl.BlockSpec((b, ch), lambda j: (0, j)),
            pl.BlockSpec((1,) + bd.shape[1:], lambda j: (j, 0, 0)),
            pl.BlockSpec((1,) + cd.shape[1:], lambda j: (j, 0, 0)),
            pl.BlockSpec((2, 1, cols), lambda j: (0, 0, j)),
            pl.BlockSpec((1, ch), lambda j: (0, j)),
            state, state,
        ],
        out_specs=[pl.BlockSpec((b, ch), lambda j: (0, j)), state, state],
        out_shape=[jax.ShapeDtypeStruct((b, d_ssm), BF16),
                   jax.ShapeDtypeStruct(h0_re.shape, F32), jax.ShapeDtypeStruct(h0_im.shape, F32)],
        compiler_params=_cparams(("parallel",)),
    )(u, bd, cd, lam_row, d_row, h0_re, h0_im)


def _log_keep(z):
    return -(jnp.maximum(z, 0.0) + jnp.log(1.0 + jnp.exp(-jnp.abs(z))))


def _split_bf16(x):
    hi = x.astype(BF16)
    lo = (x - hi.astype(F32)).astype(BF16)
    return jnp.concatenate([hi, lo], axis=-1)


def _sb_prompt_body(bias_ref, q_ref, k_ref, v_ref, o_ref, *, scale, heads_per_step):
    hp = pl.program_id(1)
    qi = pl.program_id(2)
    tq = q_ref.shape[0]
    dh = SB_HEAD_DIM
    r = lax.broadcasted_iota(jnp.int32, (tq, tq), 0)
    c = lax.broadcasted_iota(jnp.int32, (tq, tq), 1)
    below = r > c
    tri = jnp.where(below, 1.0, 0.0).astype(BF16)
    tri2 = jnp.concatenate([tri, tri], axis=0)
    nh = heads_per_step
    qs = [q_ref[:, hh * dh:(hh + 1) * dh] for hh in range(nh)]
    biases = [bias_ref[hp * nh + hh] for hh in range(nh)]

    def scores(kb):
        rows = pl.ds(pl.multiple_of(kb * tq, tq), tq)
        return tuple(lax.dot_general(qs[hh], k_ref[rows, hh * dh:(hh + 1) * dh], (((1,), (1,)), ((), ())),
                                     preferred_element_type=F32) for hh in range(nh))

    def block(kb, zs, carry, diagonal):
        rows = pl.ds(pl.multiple_of(kb * tq, tq), tq)
        z = [zs[hh] * scale + biases[hh] for hh in range(nh)]
        lk = [_log_keep(z[hh]) for hh in range(nh)]
        log_beta = [z[hh] + lk[hh] for hh in range(nh)]
        if diagonal:
            lk = [jnp.where(below, lk[hh], 0.0) for hh in range(nh)]
        after = [jnp.dot(_split_bf16(lk[hh]), tri2, preferred_element_type=F32) for hh in range(nh)]
        w = [jnp.exp(log_beta[hh] + after[hh] + carry[hh][1]) for hh in range(nh)]
        if diagonal:
            w = [jnp.where(below, w[hh], 0.0) for hh in range(nh)]
        new = []
        for hh in range(nh):
            acc = carry[hh][0] + jnp.dot(w[hh].astype(BF16), v_ref[rows, hh * dh:(hh + 1) * dh],
                                         preferred_element_type=F32)
            run = carry[hh][1] + jnp.sum(lk[hh], axis=-1, keepdims=True)
            new.append((acc, run))
        return tuple(new)

    zero = tuple((jnp.zeros((tq, dh), F32), jnp.zeros((tq, 1), F32)) for _ in range(nh))
    carry = block(qi, scores(qi), zero, True)
    carry = lax.fori_loop(0, qi, lambda i, cr: block(qi - 1 - i, scores(qi - 1 - i), cr, False), carry)
    for hh in range(nh):
        o_ref[:, hh * dh:(hh + 1) * dh] = carry[hh][0].astype(o_ref.dtype)


def _sb_prompt(q, k, v, bias, n_seq, *, tq):
    m, hd = q.shape
    t = m // n_seq
    hps = SB_PROMPT_HEADS_PER_STEP
    heads = hd // SB_HEAD_DIM
    assert t % tq == 0 and heads % hps == 0
    nq = t // tq
    kv_spec = pl.BlockSpec((t, hps * SB_HEAD_DIM), lambda n, h, i: (n, h))
    q_spec = pl.BlockSpec((tq, hps * SB_HEAD_DIM), lambda n, h, i: (n * nq + i, h))
    return pl.pallas_call(
        functools.partial(_sb_prompt_body, scale=SB_HEAD_DIM ** -0.5, heads_per_step=hps),
        name="sb_prompt",
        grid=(n_seq, heads // hps, nq),
        in_specs=[pl.BlockSpec(memory_space=pltpu.SMEM), q_spec, kv_spec, kv_spec],
        out_specs=q_spec,
        out_shape=jax.ShapeDtypeStruct((m, hd), BF16),
        compiler_params=_cparams(("parallel", "parallel", "arbitrary")),
    )(bias.astype(F32), q, k, v)


def _sb_sample_body(pt_ref, bias_ref, q_ref, knew_ref, vnew_ref, *refs, scale, n_pp, new_key_visible):
    k_refs, v_refs = refs[:n_pp], refs[n_pp:2 * n_pp]
    o_ref, acc, run = refs[2 * n_pp:]
    p = pl.program_id(1)
    heads, dh = q_ref.shape
    page = k_refs[0].shape[0]
    ncol = page * heads
    tile = MXU_DIM
    bias = bias_ref[...]
    q = q_ref[...]

    row = lax.broadcasted_iota(jnp.int32, (heads, ncol), 0)
    col = lax.broadcasted_iota(jnp.int32, (heads, ncol), 1)
    own = lax.rem(col, jnp.full_like(col, heads)) == row
    tr = lax.broadcasted_iota(jnp.int32, (tile, tile), 0)
    tc = lax.broadcasted_iota(jnp.int32, (tile, tile), 1)
    later = lax.div(tr, jnp.full_like(tr, heads)) > lax.div(tc, jnp.full_like(tc, heads))
    tri = jnp.where(later, 1.0, 0.0).astype(BF16)
    tri2 = jnp.concatenate([tri, tri], axis=0)

    @pl.when(p == 0)
    def _():
        z_new = jnp.sum(q.astype(F32) * knew_ref[...], axis=-1, keepdims=True) * scale + bias
        lk_new = _log_keep(z_new)
        w_new = jnp.exp(z_new + lk_new)
        if not new_key_visible:
            lk_new = jnp.zeros_like(lk_new)
            w_new = jnp.zeros_like(w_new)
        run[...] = lk_new
        acc[...] = w_new * vnew_ref[...]

    nt = ncol // tile
    v2 = [v_refs[i][...].reshape(ncol, dh).astype(BF16) for i in range(n_pp)]
    z = [lax.dot_general(q, k_refs[i][...].reshape(ncol, dh).astype(BF16), (((1,), (1,)), ((), ())),
                         preferred_element_type=F32) * scale + bias for i in range(n_pp)]
    lk_raw = [_log_keep(z[i]) for i in range(n_pp)]
    log_beta = [z[i] + lk_raw[i] for i in range(n_pp)]
    lk = [jnp.where(own, lk_raw[i], 0.0) for i in range(n_pp)]
    after = [[jnp.dot(_split_bf16(lk[i][:, t * tile:(t + 1) * tile]), tri2, preferred_element_type=F32)
              for t in range(nt)] for i in range(n_pp)]
    total = [[jnp.sum(lk[i][:, t * tile:(t + 1) * tile], axis=-1, keepdims=True) for t in range(nt)]
             for i in range(n_pp)]
    carry = run[...]
    out = acc[...]
    for i in range(n_pp):
        ws = [None] * nt
        for t in reversed(range(nt)):
            cols = slice(t * tile, (t + 1) * tile)
            ws[t] = jnp.exp(log_beta[i][:, cols] + after[i][t] + carry)
            carry = carry + total[i][t]
        w = jnp.where(own, jnp.concatenate(ws, axis=-1), 0.0)
        out = out + jnp.dot(w.astype(BF16), v2[i], preferred_element_type=F32)
    run[...] = carry
    acc[...] = out

    @pl.when(p == pl.num_programs(1) - 1)
    def _():
        o_ref[...] = out.astype(o_ref.dtype)


def _sb_sample(q, k_new, v_new, bias, pool_k, pool_v, page_table, layer):
    b, heads, dh = q.shape
    page = pool_k.shape[2]
    n_pages = page_table.shape[1]
    n_pp = SB_SAMPLE_PAGES_PER_STEP
    assert n_pages % n_pp == 0 and (page * heads) % MXU_DIM == 0 and MXU_DIM % heads == 0
    past = n_pages * page
    new_key_pos, query_pos = past, past
    new_key_visible = new_key_pos < query_pos
    row = pl.BlockSpec((None, heads, dh), lambda n, p, pt: (n, 0, 0))

    def page_spec(i):
        return pl.BlockSpec(
            (None, None, page, heads, dh),
            lambda n, p, pt: (layer, pt[n * n_pages + n_pages - 1 - (p * n_pp + i)], 0, 0, 0))

    grid_spec = pltpu.PrefetchScalarGridSpec(
        num_scalar_prefetch=1,
        grid=(b, n_pages // n_pp),
        in_specs=[pl.BlockSpec((heads, 1), lambda n, p, pt: (0, 0)), row, row, row]
                 + [page_spec(i) for i in range(n_pp)] * 2,
        out_specs=row,
        scratch_shapes=[pltpu.VMEM((heads, dh), F32), pltpu.VMEM((heads, 1), F32)],
    )
    return pl.pallas_call(
        functools.partial(_sb_sample_body, scale=SB_HEAD_DIM ** -0.5, n_pp=n_pp, new_key_visible=new_key_visible),
        name="sb_sample",
        grid_spec=grid_spec,
        out_shape=jax.ShapeDtypeStruct((b, heads, dh), BF16),
        compiler_params=_cparams(("parallel", "arbitrary")),
    )(page_table.reshape(-1).astype(jnp.int32), bias.astype(F32).reshape(heads, 1), q, k_new, v_new,
      *([pool_k] * n_pp), *([pool_v] * n_pp))


def _xattn_prompt_body(q_ref, k_ref, v_ref, o_ref, *, scale):
    q = q_ref[...]
    kb = k_ref[...].astype(BF16)
    vb = v_ref[...].astype(BF16)
    s = lax.dot_general(q, kb, (((1,), (1,)), ((), ())), preferred_element_type=F32) * scale
    s = s - jnp.max(s, axis=-1, keepdims=True)
    e = jnp.exp(s)
    p = e / jnp.sum(e, axis=-1, keepdims=True)
    o_ref[...] = jnp.dot(p.astype(BF16), vb, preferred_element_type=F32).astype(o_ref.dtype)


def _xattn_prompt(q, mem_k, mem_v, n_seq, *, tq):
    m, d = q.shape
    t = m // n_seq
    hd = d // MEM_HEADS
    n_mem = mem_k.shape[0] // n_seq
    nq = t // tq
    q_spec = pl.BlockSpec((tq, hd), lambda n, h, i: (n * nq + i, h))
    kv_spec = pl.BlockSpec((n_mem, hd), lambda n, h, i: (n, h))
    return pl.pallas_call(
        functools.partial(_xattn_prompt_body, scale=hd ** -0.5),
        name="xattn_prompt",
        grid=(n_seq, MEM_HEADS, nq),
        in_specs=[q_spec, kv_spec, kv_spec],
        out_specs=q_spec,
        out_shape=jax.ShapeDtypeStruct((m, d), BF16),
        compiler_params=_cparams(("parallel", "parallel", "parallel")),
    )(q, mem_k, mem_v)


def _xattn_sample_body(q_ref, k_ref, v_ref, o_ref, *, scale):
    for i in range(q_ref.shape[0]):
        q = q_ref[i]
        s = jnp.sum(k_ref[i] * q[None], axis=-1, keepdims=True) * scale
        s = s - jnp.max(s, axis=0, keepdims=True)
        e = jnp.exp(s)
        p = e / jnp.sum(e, axis=0, keepdims=True)
        o_ref[i] = jnp.sum(p * v_ref[i], axis=0)


def _xattn_sample(q, mem_k, mem_v, layer):
    b, heads, hd = q.shape
    n_mem = mem_k.shape[2]
    per_step = XATTN_SAMPLE_SEQS_PER_STEP
    assert b % per_step == 0
    row = pl.BlockSpec((per_step, heads, hd), lambda n: (n, 0, 0))
    kv = pl.BlockSpec((None, per_step, n_mem, heads, hd), lambda n: (layer, n, 0, 0, 0))
    return pl.pallas_call(
        functools.partial(_xattn_sample_body, scale=hd ** -0.5),
        name="xattn_sample",
        grid=(b // per_step,),
        in_specs=[row, kv, kv],
        out_specs=row,
        out_shape=jax.ShapeDtypeStruct((b, heads, hd), F32),
        compiler_params=_cparams(("parallel",)),
    )(q, mem_k, mem_v)


def _layer(x, w, *, tag, tm, tn, first_tiles, wide_tiles, ff2_tiles, sb_fn, s5_fn, xattn_fn, alpha):
    m, d = x.shape
    d_ssm = w['w_glu_val'].shape[0]
    d_sb = w['w_sb_branch'].shape[0]

    def mm(a, bs, name, tiles=None, **kw):
        return _matmul(a, bs, name=f"{tag}_{name}", **(tiles or dict(tm=tm, tn=tn)), **kw)

    w_in = w['w_in']
    u, xb = mm(x, [w_in], "in_u", tiles=first_tiles, n=d_ssm, b_col=0, epilogue=_epi_plain, out_dtypes=[F32],
               emit_a=True)
    q, = mm(xb, [w_in], "in_q", n=d_sb, b_col=d_ssm, epilogue=_epi_plain, out_dtypes=[BF16])
    k, kb = mm(xb, [w_in], "in_k", n=d_sb, b_col=d_ssm + d_sb, epilogue=_epi_two_copies, out_dtypes=[F32, BF16])
    v, vb = mm(xb, [w_in], "in_v", n=d_sb, b_col=d_ssm + 2 * d_sb, epilogue=_epi_two_copies,
               out_dtypes=[F32, BF16])
    gates, = mm(xb, [w_in], "in_gates", tiles=wide_tiles, n=2 * d, b_col=d_ssm + 3 * d_sb, epilogue=_epi_sigmoid,
                out_dtypes=[F32])

    y, s5_state = s5_fn(u)
    ssm, = mm(y, [w['w_glu_val'], w['w_glu_gate']], "glu", n=d, epilogue=_epi_glu,
              extras=[(gates, 'tile', 0)], out_dtypes=[F32])
    sb = sb_fn(q, k, v, kb, vb)
    merged, = mm(sb, [w['w_sb_branch']], "merge", n=d, epilogue=_epi_merge,
                 extras=[(ssm, 'tile', 0), (gates, 'tile', d)], out_dtypes=[BF16])
    residual = functools.partial(_epi_residual, alpha=alpha)
    s1, = mm(merged, [w['w_mix_out']], "mix_out", n=d, epilogue=residual, extras=[(x, 'tile', 0)],
             out_dtypes=[F32])
    ln_tm = min(tm, 256)
    x1, x1b = _layer_norm(s1, w['ln1_g'], w['ln1_b'], name=f"{tag}_ln1", tm=ln_tm, out_dtypes=[F32, BF16])

    qm, = mm(x1b, [w['w_mem_q']], "mem_q", n=d, epilogue=_epi_plain, out_dtypes=[BF16])
    o = xattn_fn(qm)
    s2, = mm(o, [w['w_mem_o']], "mem_o", n=d, epilogue=residual, extras=[(x1, 'tile', 0)], out_dtypes=[F32])
    x2, x2b = _layer_norm(s2, w['ln2_g'], w['ln2_b'], name=f"{tag}_ln2", tm=ln_tm, out_dtypes=[F32, BF16])

    hmid, = mm(x2b, [w['w_ff1']], "ff1", tiles=wide_tiles, n=w['w_ff1'].shape[1], epilogue=_epi_relu2,
               out_dtypes=[BF16])
    s3, = _matmul(hmid, [w['w_ff2_bf16']], name=f"{tag}_ff2", n=d, **ff2_tiles, epilogue=residual,
                  extras=[(x2, 'tile', 0)], out_dtypes=[F32])
    x3, = _layer_norm(s3, w['ln3_g'], w['ln3_b'], name=f"{tag}_ln3", tm=ln_tm, out_dtypes=[F32])
    return x3, k, v, s5_state


def kernel(x_prompt, x_sample, cache_k, cache_v, cache_mem_k, cache_mem_v, state_ssm_re, state_ssm_im, page_table, mem_prompt, w_in, ssm_a_re, ssm_a_im, ssm_log_dt, ssm_b_re, ssm_b_im, ssm_c_re, ssm_c_im, ssm_d, w_glu_val, w_glu_gate, sb_bias, w_sb_branch, w_mix_out, ln1_g, ln1_b, w_mem_q, w_mem_k, w_mem_v, w_mem_o, ln2_g, ln2_b, w_ff1, w_ff2, ln3_g, ln3_b):
    depth = w_in.shape[0]
    n_seq, seq, d = x_prompt.shape
    b_dec, dec_seq, _ = x_sample.shape
    assert dec_seq == 1, "the decode group is one token per sequence"
    n_mem = mem_prompt.shape[1]
    alpha = (2 * depth) ** 0.25
    heads = sb_bias.shape[1]
    groups = ssm_a_re.shape[1]

    xp = x_prompt.reshape(n_seq * seq, d)
    xs = x_sample.reshape(b_dec, d)
    memb = mem_prompt.reshape(n_seq * n_mem, d).astype(BF16)
    outs = {name: [] for name in ('kp', 'vp', 'hrp', 'hip', 'mkp', 'mvp', 'ks', 'vs', 'hrs', 'his')}
    for l in range(depth):
        w = dict(w_in=w_in[l], w_glu_val=w_glu_val[l], w_glu_gate=w_glu_gate[l], w_sb_branch=w_sb_branch[l],
                 w_mix_out=w_mix_out[l], ln1_g=ln1_g[l], ln1_b=ln1_b[l], w_mem_q=w_mem_q[l], w_mem_o=w_mem_o[l],
                 ln2_g=ln2_g[l], ln2_b=ln2_b[l], w_ff1=w_ff1[l], w_ff2_bf16=w_ff2[l].astype(BF16),
                 ln3_g=ln3_g[l], ln3_b=ln3_b[l])
        lb_re, lb_im, bb_re, bb_im = _s5_discretise(ssm_a_re[l], ssm_a_im[l], ssm_log_dt[l], ssm_b_re[l], ssm_b_im[l])
        bd, cd, lam_scan, lam_row, d_row = _s5_layout(lb_re, lb_im, bb_re, bb_im, ssm_c_re[l], ssm_c_im[l], ssm_d[l])

        mem_tiles = dict(n=d, tm=n_seq * n_mem, tn=256, epilogue=_epi_plain, out_dtypes=[F32])
        mkp, = _matmul(memb, [w_mem_k[l]], name="prompt_mem_k", **mem_tiles)
        mvp, = _matmul(memb, [w_mem_v[l]], name="prompt_mem_v", **mem_tiles)

        def s5_prompt(u):
            y, hre, him = _s5_prompt(u, n_seq, bd, cd, lam_scan, d_row)
            return y, (hre, him)

        xp, kp, vp, (hrp, hip) = _layer(
            xp, w, tag="prompt", tm=2048, tn=256, first_tiles=dict(tm=512, tn=256),
            wide_tiles=dict(tm=1024, tn=512), ff2_tiles=dict(tm=512, tn=256), alpha=alpha, s5_fn=s5_prompt,
            sb_fn=lambda q, k, v, kb, vb: _sb_prompt(q, kb, vb, sb_bias[l], n_seq, tq=256),
            xattn_fn=lambda qm: _xattn_prompt(qm, mkp, mvp, n_seq, tq=512))

        def s5_sample(u):
            y, hre, him = _s5_sample(u, state_ssm_re[l].reshape(b_dec, -1), state_ssm_im[l].reshape(b_dec, -1),
                                     bd, cd, lam_row, d_row)
            return y, (hre.reshape(b_dec, groups, -1), him.reshape(b_dec, groups, -1))

        def sb_sample(q, k, v, kb, vb):
            per_head = lambda a: a.reshape(b_dec, heads, -1)
            return _sb_sample(per_head(q), per_head(k), per_head(v), sb_bias[l], cache_k, cache_v,
                              page_table, l).reshape(b_dec, -1)

        def xattn_sample(qm):
            o = _xattn_sample(qm.astype(F32).reshape(b_dec, MEM_HEADS, -1), cache_mem_k, cache_mem_v, l)
            return o.reshape(b_dec, d).astype(BF16)

        xs, ks, vs, (hrs, his) = _layer(
            xs, w, tag="sample", tm=b_dec, tn=512, first_tiles=dict(tm=b_dec, tn=512),
            wide_tiles=dict(tm=b_dec, tn=512), ff2_tiles=dict(tm=b_dec, tn=512), alpha=alpha,
            s5_fn=s5_sample, sb_fn=sb_sample, xattn_fn=xattn_sample)

        outs['kp'].append(kp.reshape(n_seq, seq, heads, -1))
        outs['vp'].append(vp.reshape(n_seq, seq, heads, -1))
        outs['hrp'].append(hrp)
        outs['hip'].append(hip)
        outs['mkp'].append(mkp.reshape(n_seq, n_mem, MEM_HEADS, -1))
        outs['mvp'].append(mvp.reshape(n_seq, n_mem, MEM_HEADS, -1))
        outs['ks'].append(ks.reshape(b_dec, dec_seq, heads, -1))
        outs['vs'].append(vs.reshape(b_dec, dec_seq, heads, -1))
        outs['hrs'].append(hrs)
        outs['his'].append(his)

    st = {name: jnp.stack(vals) for name, vals in outs.items()}
    return (xp.reshape(n_seq, seq, d), xs.reshape(b_dec, dec_seq, d),
            st['kp'], st['vp'], st['hrp'], st['hip'], st['mkp'], st['mvp'],
            st['ks'], st['vs'], st['hrs'], st['his'])
```
